```python
import math
import jax, jax.numpy as jnp
from jax import lax
import numpy as np

D_MODEL = 2048
BATCH = 16
SEQ = 2048
DEPTH = 4

HEAD_DIM = 64
A_HEADS = 16
A_KV_HEADS = 2
WINDOW = 128
BLOCK = 128
B_HEADS = 16
KV_RANK = 256
IDX_HEADS = 16
IDX_DIM = 64
TOPK_MAX = 256
NUM_BUCKETS = 32
MAX_DISTANCE = 128
D_FF = 5632
CONV_WIDTH = 3
EPS = 1e-6
NEG_INF = -1e30
IDX_SCALE = IDX_DIM ** -0.5 * IDX_HEADS ** -0.5
MIX_WIDTH = (A_HEADS + B_HEADS) * HEAD_DIM

A_Q = A_HEADS * HEAD_DIM
A_KV = A_KV_HEADS * HEAD_DIM
B_Q = B_HEADS * HEAD_DIM
I_Q = IDX_HEADS * IDX_DIM
IN_WIDTH = A_Q + 2 * A_KV + B_Q + KV_RANK + I_Q + IDX_DIM + IDX_HEADS
SPLIT_POINTS = (
    A_Q,
    A_Q + A_KV,
    A_Q + 2 * A_KV,
    A_Q + 2 * A_KV + B_Q,
    A_Q + 2 * A_KV + B_Q + KV_RANK,
    A_Q + 2 * A_KV + B_Q + KV_RANK + I_Q,
    A_Q + 2 * A_KV + B_Q + KV_RANK + I_Q + IDX_DIM,
)

kernel_name = "hymba_swa_dsa_convffn_adaln"


def rms_norm(x, g):
    xf = x.astype(jnp.float32)
    y = xf * lax.rsqrt(jnp.mean(xf * xf, axis=-1, keepdims=True) + EPS)
    return (y * g.astype(jnp.float32)).astype(x.dtype)


def rel_bucket(dist):
    n = jnp.maximum(dist, 0)
    max_exact = NUM_BUCKETS // 2
    nf = jnp.maximum(n, 1).astype(jnp.float32)
    large = max_exact + (jnp.log(nf / max_exact) / math.log(MAX_DISTANCE / max_exact)
                         * (NUM_BUCKETS - max_exact)).astype(jnp.int32)
    large = jnp.minimum(large, NUM_BUCKETS - 1)
    return jnp.where(n < max_exact, n, large)


def swa_attention(q, k, v, sinks, rel_table):
    Bsz, T, H, D = q.shape
    nblk = T // BLOCK
    G = H // A_KV_HEADS
    qb = q.reshape(Bsz, nblk, BLOCK, A_KV_HEADS, G, D)
    pad = ((0, 0), (BLOCK, 0), (0, 0), (0, 0))
    kp = jnp.pad(k, pad).reshape(Bsz, nblk + 1, BLOCK, A_KV_HEADS, D)
    vp = jnp.pad(v, pad).reshape(Bsz, nblk + 1, BLOCK, A_KV_HEADS, D)
    kb = jnp.concatenate([kp[:, :-1], kp[:, 1:]], axis=2)
    vb = jnp.concatenate([vp[:, :-1], vp[:, 1:]], axis=2)
    s = jnp.einsum('bnqkgd,bnskd->bnkgqs', qb, kb).astype(jnp.float32) * (D ** -0.5)
    i = jnp.arange(BLOCK, dtype=jnp.int32)[:, None]
    j = jnp.arange(2 * BLOCK, dtype=jnp.int32)[None, :]
    dist = i + BLOCK - j
    in_window = (dist >= 0) & (dist < WINDOW)
    key_pos = jnp.arange(nblk, dtype=jnp.int32)[:, None, None] * BLOCK - BLOCK + j[None]
    valid = in_window[None] & (key_pos >= 0)
    bias = jnp.transpose(rel_table[rel_bucket(dist)], (2, 0, 1))
    bias = bias.reshape(A_KV_HEADS, G, BLOCK, 2 * BLOCK).astype(jnp.float32)
    s = jnp.where(valid[None, :, None, None], s + bias[None, None], NEG_INF)
    sink = sinks.astype(jnp.float32).reshape(A_KV_HEADS, G)[None, None, :, :, None, None]
    m = jnp.maximum(jnp.max(s, axis=-1, keepdims=True), sink)
    e = jnp.exp(s - m)
    p = e / (jnp.sum(e, axis=-1, keepdims=True) + jnp.exp(sink - m))
    o = jnp.einsum('bnkgqs,bnskd->bnqkgd', p.astype(v.dtype), vb)
    return o.reshape(Bsz, T, H * D)


def dsa_attention(q, c_kv, w_uk, w_uv, q_idx, k_idx, w_idx, rel_table):
    Bsz, T, H, D = q.shape
    nblk = T // BLOCK
    topk = min(TOPK_MAX, T // 4)
    scale = D ** -0.5
    key_pos = jnp.arange(T, dtype=jnp.int32)

    def one_block(blk):
        start = blk * BLOCK
        qpos = start + jnp.arange(BLOCK, dtype=jnp.int32)
        qi = lax.dynamic_slice_in_dim(q_idx, start, BLOCK, axis=1)
        wi = lax.dynamic_slice_in_dim(w_idx, start, BLOCK, axis=1)
        qb = lax.dynamic_slice_in_dim(q, start, BLOCK, axis=1)
        dots = jnp.einsum('bqhd,bsd->bqhs', qi, k_idx).astype(jnp.float32)
        index_score = jnp.einsum('bqh,bqhs->bqs', wi.astype(jnp.float32),
                                 jax.nn.relu(dots)) * IDX_SCALE
        causal = key_pos[None, :] <= qpos[:, None]
        index_score = jnp.where(causal[None], index_score, NEG_INF)
        _, sel = lax.top_k(index_score, topk)
        c_sel = jax.vmap(lambda ck, ix: ck[ix])(c_kv, sel)
        q_lat = jnp.einsum('bqhd,hrd->bqhr', qb, w_uk)
        s = jnp.einsum('bqhr,bqkr->bqhk', q_lat, c_sel).astype(jnp.float32) * scale
        dist = qpos[None, :, None] - sel
        bias = jnp.moveaxis(rel_table[rel_bucket(dist)], -1, 2).astype(jnp.float32)
        s = jnp.where((dist >= 0)[:, :, None, :], s + bias, NEG_INF)
        p = jax.nn.softmax(s, axis=-1).astype(c_sel.dtype)
        o_lat = jnp.einsum('bqhk,bqkr->bqhr', p, c_sel)
        o = jnp.einsum('bqhr,hrd->bqhd', o_lat, w_uv)
        return o.reshape(Bsz, BLOCK, H * D)

    out = lax.map(one_block, jnp.arange(nblk, dtype=jnp.int32))
    return jnp.transpose(out, (1, 0, 2, 3)).reshape(Bsz, T, H * D)


def conv_ffn(h, w_up, conv_w, conv_b, w_down):
    u, g = jnp.split(h @ w_up, 2, axis=-1)
    u = lax.conv_general_dilated(
        u, conv_w[:, None, :], window_strides=(1,), padding=((CONV_WIDTH - 1, 0),),
        dimension_numbers=('NWC', 'WIO', 'NWC'), feature_group_count=D_FF) + conv_b
    return (jax.nn.silu(u) * g) @ w_down


def setup_inputs(seed: int = 0) -> dict:
    key = jax.random.key(seed)
    ks = jax.random.split(key, 20)
    f32 = jnp.float32
    nrm = lambda k, shape, s: jax.random.normal(k, shape, f32) * s
    return {
        "x": nrm(ks[0], (BATCH, SEQ, D_MODEL), 1.0),
        "c": nrm(ks[1], (BATCH, D_MODEL), 1.0),
        "w_mod": nrm(ks[2], (DEPTH, D_MODEL, 6 * D_MODEL), D_MODEL ** -0.5),
        "b_mod": nrm(ks[3], (DEPTH, 6 * D_MODEL), 0.02),
        "norm_attn": 1.0 + nrm(ks[4], (DEPTH, D_MODEL), 0.02),
        "w_in": nrm(ks[5], (DEPTH, D_MODEL, IN_WIDTH), D_MODEL ** -0.5),
        "attn_sinks": nrm(ks[6], (DEPTH, A_HEADS), 0.5),
        "kv_norm": 1.0 + nrm(ks[7], (DEPTH, KV_RANK), 0.02),
        "w_uk": nrm(ks[8], (DEPTH, B_HEADS, KV_RANK, HEAD_DIM), KV_RANK ** -0.5),
        "w_uv": nrm(ks[9], (DEPTH, B_HEADS, KV_RANK, HEAD_DIM), KV_RANK ** -0.5),
        "rel_bias": nrm(ks[10], (NUM_BUCKETS, A_HEADS + B_HEADS), 0.3),
        "w_out": nrm(ks[11], (DEPTH, MIX_WIDTH, D_MODEL), MIX_WIDTH ** -0.5),
        "norm_ffn": 1.0 + nrm(ks[12], (DEPTH, D_MODEL), 0.02),
        "w_up": nrm(ks[13], (DEPTH, D_MODEL, 2 * D_FF), D_MODEL ** -0.5),
        "conv_w": nrm(ks[14], (DEPTH, CONV_WIDTH, D_FF), CONV_WIDTH ** -0.5),
        "conv_b": nrm(ks[15], (DEPTH, D_FF), 0.02),
        "w_down": nrm(ks[16], (DEPTH, D_FF, D_MODEL), D_FF ** -0.5),
        "norm_final": 1.0 + nrm(ks[17], (D_MODEL,), 0.02),
    }


def reference(x, c, w_mod, b_mod, norm_attn, w_in, attn_sinks, kv_norm, w_uk, w_uv,
              rel_bias, w_out, norm_ffn, w_up, conv_w, conv_b, w_down, norm_final):
    Bsz, T, _ = x.shape
    rel_a = rel_bias[:, :A_HEADS]
    rel_b = rel_bias[:, A_HEADS:]
    c_act = jax.nn.silu(c)
    for l in range(DEPTH):
        mod = c_act @ w_mod[l] + b_mod[l]
        sh1, sc1, g1, sh2, sc2, g2 = [m[:, None, :] for m in jnp.split(mod, 6, axis=-1)]

        h = rms_norm(x, norm_attn[l]) * (1.0 + sc1) + sh1
        qa, ka, va, qb, ckv, qi, ki, wi = jnp.split(h @ w_in[l], SPLIT_POINTS, axis=-1)
        qa = qa.reshape(Bsz, T, A_HEADS, HEAD_DIM)
        ka = ka.reshape(Bsz, T, A_KV_HEADS, HEAD_DIM)
        va = va.reshape(Bsz, T, A_KV_HEADS, HEAD_DIM)
        qb = qb.reshape(Bsz, T, B_HEADS, HEAD_DIM)
        ckv = rms_norm(ckv, kv_norm[l])
        qi = qi.reshape(Bsz, T, IDX_HEADS, IDX_DIM)
        out_a = swa_attention(qa, ka, va, attn_sinks[l], rel_a)
        out_b = dsa_attention(qb, ckv, w_uk[l], w_uv[l], qi, ki, wi, rel_b)
        mix = jnp.concatenate([out_a, out_b], axis=-1) @ w_out[l]
        x = x + g1 * mix

        h = rms_norm(x, norm_ffn[l]) * (1.0 + sc2) + sh2
        x = x + g2 * conv_ffn(h, w_up[l], conv_w[l], conv_b[l], w_down[l])
    return rms_norm(x, norm_final)
```

```python
import functools
import math

import jax
import jax.numpy as jnp
import numpy as np
from jax import lax
from jax.experimental import pallas as pl
from jax.experimental.pallas import tpu as pltpu

HEAD_DIM = 64
A_HEADS = 16
A_KV_HEADS = 2
BLOCK = 128
B_HEADS = 16
KV_RANK = 256
IDX_HEADS = 16
IDX_DIM = 64
TOPK_MAX = 256
NUM_BUCKETS = 32
MAX_DISTANCE = 128
EPS = 1e-6
NEG_INF = -1e30
IDX_SCALE = IDX_DIM ** -0.5 * IDX_HEADS ** -0.5
QK_SCALE = HEAD_DIM ** -0.5

A_Q = A_HEADS * HEAD_DIM
A_KV = A_KV_HEADS * HEAD_DIM
B_Q = B_HEADS * HEAD_DIM
I_Q = IDX_HEADS * IDX_DIM

LANES = 128
KCHUNK = 256
HALO = 16
VMEM_LIMIT = 56 * 1024 * 1024

C_QA = 0
C_QB = C_QA + A_Q
C_QI = C_QB + B_Q
C_CKV = C_QI + I_Q
C_KA = C_CKV + KV_RANK
C_VA = C_KA + 2 * A_KV
C_KI = C_VA + 2 * A_KV
C_WI = C_KI + 2 * IDX_DIM
IN_PACKED = C_WI + LANES

bf16 = jnp.bfloat16
f32 = jnp.float32


def _cparams(sem):
    return pltpu.CompilerParams(dimension_semantics=sem, vmem_limit_bytes=VMEM_LIMIT)


def _dot(a, b):
    return jnp.dot(a, b, preferred_element_type=f32)


def _dot_nt(a, b):
    return lax.dot_general(a, b, (((1,), (1,)), ((), ())), preferred_element_type=f32)


def _sigmoid(x):
    return 1.0 / (1.0 + jnp.exp(-x))


def _order_key(bits):
    return bits ^ ((bits >> 31) & jnp.int32(0x7FFFFFFF))


_NEG_BITS = int(np.float32(NEG_INF).view(np.int32))
NEG_KEY = _NEG_BITS ^ ((_NEG_BITS >> 31) & 0x7FFFFFFF)


def _rms_mod(x, g, sc, sh):
    ms = jnp.mean(x * x, axis=-1, keepdims=True)
    y = x * lax.rsqrt(ms + EPS) * g
    return y * (1.0 + sc) + sh


def _mod_kernel(c_ref, w_ref, b_ref, o_ref):
    c = c_ref[...]
    ca = (c * _sigmoid(c)).astype(bf16)
    o_ref[...] = _dot(ca, w_ref[...].astype(bf16)) + b_ref[...]


def _modulation(c, w_mod, b_mod):
    depth, d, n6 = w_mod.shape
    bsz = c.shape[0]
    tn = 1536 if n6 % 1536 == 0 else n6
    return pl.pallas_call(
        _mod_kernel,
        grid=(depth, n6 // tn),
        in_specs=[
            pl.BlockSpec((bsz, d), lambda l, j: (0, 0)),
            pl.BlockSpec((None, d, tn), lambda l, j: (l, 0, j)),
            pl.BlockSpec((None, 1, tn), lambda l, j: (l, 0, j)),
        ],
        out_specs=pl.BlockSpec((None, bsz, tn), lambda l, j: (l, 0, j)),
        out_shape=jax.ShapeDtypeStruct((depth, bsz, n6), f32),
        compiler_params=_cparams(("arbitrary", "arbitrary")),
        name="modulation",
    )(c, w_mod, b_mod.reshape(depth, 1, n6))


def _inproj_kernel(x_ref, sc_ref, sh_ref, g_ref, w_ref, kvg_ref,
                   qa_ref, qb_ref, qi_ref, ckv_ref, ka_ref, va_ref, ki_ref, wi_ref):
    h = _rms_mod(x_ref[...], g_ref[...], sc_ref[0], sh_ref[0]).astype(bf16)

    def proj(c0, c1):
        return _dot(h, w_ref[:, c0:c1])

    qa_ref[...] = proj(C_QA, C_QB).astype(bf16)
    qb_ref[...] = proj(C_QB, C_QI).astype(bf16)
    qi_ref[...] = proj(C_QI, C_CKV).astype(bf16)
    ckv = proj(C_CKV, C_KA)
    ckv = ckv * lax.rsqrt(jnp.mean(ckv * ckv, axis=-1, keepdims=True) + EPS) * kvg_ref[...]
    ckv_ref[...] = ckv.astype(bf16)
    ka_ref[...] = proj(C_KA, C_VA).astype(bf16)
    va_ref[...] = proj(C_VA, C_KI).astype(bf16)
    ki_ref[...] = proj(C_KI, C_WI).astype(bf16)
    wi_ref[...] = proj(C_WI, IN_PACKED)


def _inproj(x2, sc, sh, g, w_packed, kvg, l, seq, tm):
    n, d = x2.shape
    rows = lambda w: pl.BlockSpec((tm, w), lambda i: (i, 0))
    per_b = pl.BlockSpec((1, 1, d), lambda i: ((i * tm) // seq, 0, 0))
    widths = (A_Q, B_Q, I_Q, KV_RANK, 2 * A_KV, 2 * A_KV, 2 * IDX_DIM, LANES)
    dtypes = (bf16,) * 7 + (f32,)
    return pl.pallas_call(
        _inproj_kernel,
        grid=(n // tm,),
        in_specs=[
            rows(d), per_b, per_b,
            pl.BlockSpec((None, 1, d), lambda i: (l, 0, 0)),
            pl.BlockSpec((None, d, IN_PACKED), lambda i: (l, 0, 0), pipeline_mode=pl.Buffered(1)),
            pl.BlockSpec((None, 1, KV_RANK), lambda i: (l, 0, 0)),
        ],
        out_specs=[rows(w) for w in widths],
        out_shape=[jax.ShapeDtypeStruct((n, w), dt) for w, dt in zip(widths, dtypes)],
        compiler_params=_cparams(("arbitrary",)),
        name="inproj",
    )(x2, sc, sh, g, w_packed, kvg)


def _swa_kernel(q_ref, kp_ref, kc_ref, vp_ref, vc_ref, bias_ref, sink_ref, o_ref):
    i = pl.program_id(1)
    q = q_ref[0]
    kk = jnp.concatenate([kp_ref[0], kc_ref[0]], axis=0)
    vv = jnp.concatenate([vp_ref[0], vc_ref[0]], axis=0)
    lo_q = lax.broadcasted_iota(jnp.int32, (BLOCK, LANES), 1) < HEAD_DIM
    lo_v = lax.broadcasted_iota(jnp.int32, (2 * BLOCK, LANES), 1) < HEAD_DIM
    col = lax.broadcasted_iota(jnp.int32, (1, 2 * BLOCK), 1)
    no_prev = jnp.logical_and(i == 0, col < BLOCK)
    zero = jnp.zeros((), bf16)
    pairs = A_HEADS // A_KV_HEADS // 2
    half = pairs * BLOCK
    for g in range(A_KV_HEADS):
        rows = []
        for par in (0, 1):
            for pr in range(pairs):
                qp = q[:, (g * pairs + pr) * LANES:(g * pairs + pr + 1) * LANES]
                rows.append(jnp.where(lo_q, qp, zero) if par == 0 else jnp.where(lo_q, zero, qp))
        qs = jnp.concatenate(rows, axis=0)
        s = _dot_nt(qs, kk[:, g * LANES:(g + 1) * LANES]) * QK_SCALE + bias_ref[g]
        s = jnp.where(no_prev, NEG_INF, s)
        sink = sink_ref[g]
        m = jnp.maximum(jnp.max(s, axis=-1, keepdims=True), sink)
        e = jnp.exp(s - m)
        r = 1.0 / (jnp.sum(e, axis=-1, keepdims=True) + jnp.exp(sink - m))
        eb = e.astype(bf16)
        v2 = vv[:, g * LANES:(g + 1) * LANES]
        o = (_dot(eb[:half], jnp.where(lo_v, v2, zero)) * r[:half]
             + _dot(eb[half:], jnp.where(lo_v, zero, v2)) * r[half:])
        for pr in range(pairs):
            c0 = (g * pairs + pr) * LANES
            o_ref[0, :, c0:c0 + LANES] = o[pr * BLOCK:(pr + 1) * BLOCK].astype(bf16)


def _swa(qa, ka2, va2, bias, sink):
    bsz, seq, _ = qa.shape
    nblk = seq // BLOCK
    cur = lambda w: pl.BlockSpec((1, BLOCK, w), lambda b, i: (b, i, 0))
    prev = lambda w: pl.BlockSpec((1, BLOCK, w), lambda b, i: (b, jnp.maximum(i - 1, 0), 0))
    return pl.pallas_call(
        _swa_kernel,
        grid=(bsz, nblk),
        in_specs=[
            cur(A_Q), prev(2 * A_KV), cur(2 * A_KV), prev(2 * A_KV), cur(2 * A_KV),
            pl.BlockSpec(bias.shape, lambda b, i: (0, 0, 0)),
            pl.BlockSpec(sink.shape, lambda b, i: (0, 0, 0)),
        ],
        out_specs=cur(A_Q),
        out_shape=jax.ShapeDtypeStruct((bsz, seq, A_Q), bf16),
        compiler_params=_cparams(("arbitrary", "arbitrary")),
        name="swa",
    )(qa, ka2, ka2, va2, va2, bias, sink)


def _dsa_kernel(qb_ref, qi_ref, wi_ref, ki_ref, ckv_ref, wuk_ref, wuv_ref, bias_ref, o_ref,
                qstk, qlat, wb, keys, s_buf, p_buf, acc, mrun, lrun, *, topk, nch_total):
    i = pl.program_id(1)
    nch = i // 2 + 1
    rows_all = B_HEADS * BLOCK
    lo = lax.broadcasted_iota(jnp.int32, (BLOCK, LANES), 1) < HEAD_DIM
    zero = jnp.zeros((), bf16)

    qi = qi_ref[0]
    qb = qb_ref[0]
    for pr in range(B_HEADS // 2):
        qp = qi[:, pr * LANES:(pr + 1) * LANES]
        qstk[(2 * pr) * BLOCK:(2 * pr + 1) * BLOCK, :] = jnp.where(lo, qp, zero)
        qstk[(2 * pr + 1) * BLOCK:(2 * pr + 2) * BLOCK, :] = jnp.where(lo, zero, qp)
        qp = qb[:, pr * LANES:(pr + 1) * LANES]
        w = wuk_ref[pr]
        qlat[(2 * pr) * BLOCK:(2 * pr + 1) * BLOCK, :] = (
            _dot(jnp.where(lo, qp, zero), w) * QK_SCALE).astype(bf16)
        qlat[(2 * pr + 1) * BLOCK:(2 * pr + 2) * BLOCK, :] = (
            _dot(jnp.where(lo, zero, qp), w) * QK_SCALE).astype(bf16)
    wi = wi_ref[0]
    for h in range(IDX_HEADS):
        wb[h] = jnp.broadcast_to(wi[:, h:h + 1] * IDX_SCALE, (BLOCK, KCHUNK))

    qpos = i * BLOCK + lax.broadcasted_iota(jnp.int32, (BLOCK, KCHUNK), 0)
    kiota = lax.broadcasted_iota(jnp.int32, (BLOCK, KCHUNK), 1)

    def score_body(c, carry):
        start = pl.multiple_of(c * KCHUNK, KCHUNK)
        d = _dot_nt(qstk[...], ki_ref[0, pl.ds(start, KCHUNK), :])
        sc = jnp.zeros((BLOCK, KCHUNK), f32)
        for h in range(IDX_HEADS):
            sc = sc + wb[h] * jnp.maximum(d[h * BLOCK:(h + 1) * BLOCK], 0.0)
        sc = jnp.where(start + kiota <= qpos, sc, NEG_INF)
        keys[c] = _order_key(pltpu.bitcast(sc, jnp.int32))
        return carry

    lax.fori_loop(0, nch, score_body, 0)

    rest = ((nch_total - nch) * KCHUNK).astype(f32)

    def count_ge(cand):
        def body(c, cnt):
            return cnt + jnp.where(keys[c] >= cand, 1.0, 0.0)
        cnt = lax.fori_loop(0, nch, body, jnp.zeros((BLOCK, KCHUNK), f32))
        tot = jnp.sum(cnt, axis=-1, keepdims=True)
        return tot + jnp.where(cand <= NEG_KEY, rest, 0.0)

    kf = jnp.float32(topk)
    tau0 = jnp.where(count_ge(jnp.zeros((BLOCK, 1), jnp.int32)) >= kf,
                     jnp.int32(0), jnp.int32(-2 ** 31))

    def bisect_body(t, tau):
        cand = tau + lax.shift_left(jnp.int32(1), 30 - t)
        return jnp.where(count_ge(cand) >= kf, cand, tau)

    tau = lax.fori_loop(0, 31, bisect_body, tau0)

    mrun[...] = jnp.full(mrun.shape, NEG_INF, f32)

    def pass1(c, carry):
        start = pl.multiple_of(c * KCHUNK, KCHUNK)
        s_all = _dot_nt(qlat[...], ckv_ref[0, pl.ds(start, KCHUNK), :])
        sel = jnp.logical_and(keys[c] >= tau, start + kiota <= qpos)
        t0 = jnp.clip(2 * c - i + 2, 0, 2)
        t1 = jnp.clip(2 * c + 1 - i + 2, 0, 2)
        for h in range(B_HEADS):
            r0, r1 = h * BLOCK, (h + 1) * BLOCK
            bias = jnp.concatenate([bias_ref[t0, h], bias_ref[t1, h]], axis=1)
            s = jnp.where(sel, s_all[r0:r1] + bias, NEG_INF)
            s_buf[c, r0:r1, :] = s
            mrun[r0:r1, :] = jnp.maximum(mrun[r0:r1, :], jnp.maximum(s[:, :LANES], s[:, LANES:]))
        return carry

    lax.fori_loop(0, nch, pass1, 0)

    for h in range(B_HEADS):
        r0, r1 = h * BLOCK, (h + 1) * BLOCK
        mrun[r0:r1, :] = jnp.broadcast_to(jnp.max(mrun[r0:r1, :], axis=-1, keepdims=True), (BLOCK, LANES))

    acc[...] = jnp.zeros(acc.shape, f32)
    lrun[...] = jnp.zeros(lrun.shape, f32)

    def pass2(c, carry):
        start = pl.multiple_of(c * KCHUNK, KCHUNK)
        for h in range(B_HEADS):
            r0, r1 = h * BLOCK, (h + 1) * BLOCK
            m = mrun[r0:r1, :]
            p = jnp.exp(s_buf[c, r0:r1, :] - jnp.concatenate([m, m], axis=1))
            lrun[r0:r1, :] = lrun[r0:r1, :] + p[:, :LANES] + p[:, LANES:]
            p_buf[r0:r1, :] = p.astype(bf16)
        acc[...] = acc[...] + _dot(p_buf[...], ckv_ref[0, pl.ds(start, KCHUNK), :])
        return carry

    lax.fori_loop(0, nch, pass2, 0)

    for pr in range(B_HEADS // 2):
        parts = []
        for par in (0, 1):
            r0 = (2 * pr + par) * BLOCK
            l = jnp.sum(lrun[r0:r0 + BLOCK, :], axis=-1, keepdims=True)
            parts.append((acc[r0:r0 + BLOCK, :] * (1.0 / l)).astype(bf16))
        o = _dot(jnp.concatenate(parts, axis=1), wuv_ref[pr])
        o_ref[0, :, pr * LANES:(pr + 1) * LANES] = o.astype(bf16)


def _dsa(qb, qi, wi, ki2, ckv, wuk_p, wuv_p, bias_t, l, topk):
    bsz, seq, _ = qb.shape
    nblk = seq // BLOCK
    nch_total = seq // KCHUNK
    rows_all = B_HEADS * BLOCK
    cur = lambda w: pl.BlockSpec((1, BLOCK, w), lambda b, i: (b, i, 0))
    whole = lambda w: pl.BlockSpec((1, seq, w), lambda b, i: (b, 0, 0))
    kern = functools.partial(_dsa_kernel, topk=topk, nch_total=nch_total)
    return pl.pallas_call(
        kern,
        grid=(bsz, nblk),
        in_specs=[
            cur(B_Q), cur(I_Q), cur(LANES), whole(2 * IDX_DIM), whole(KV_RANK),
            pl.BlockSpec((None,) + wuk_p.shape[1:], lambda b, i: (l, 0, 0, 0)),
            pl.BlockSpec((None,) + wuv_p.shape[1:], lambda b, i: (l, 0, 0, 0)),
            pl.BlockSpec(bias_t.shape, lambda b, i: (0, 0, 0, 0)),
        ],
        out_specs=cur(B_Q),
        out_shape=jax.ShapeDtypeStruct((bsz, seq, B_Q), bf16),
        scratch_shapes=[
            pltpu.VMEM((rows_all, LANES), bf16),
            pltpu.VMEM((rows_all, KV_RANK), bf16),
            pltpu.VMEM((IDX_HEADS, BLOCK, KCHUNK), f32),
            pltpu.VMEM((nch_total, BLOCK, KCHUNK), jnp.int32),
            pltpu.VMEM((nch_total, rows_all, KCHUNK), f32),
            pltpu.VMEM((rows_all, KCHUNK), bf16),
            pltpu.VMEM((rows_all, KV_RANK), f32),
            pltpu.VMEM((rows_all, LANES), f32),
            pltpu.VMEM((rows_all, LANES), f32),
        ],
        compiler_params=_cparams(("arbitrary", "arbitrary")),
        name="dsa",
    )(qb, qi, wi, ki2, ckv, wuk_p, wuv_p, bias_t)


def _outproj_kernel(oa_ref, ob_ref, w_ref, x_ref, g_ref, o_ref):
    mix = _dot(oa_ref[...], w_ref[:A_Q, :]) + _dot(ob_ref[...], w_ref[A_Q:, :])
    o_ref[...] = x_ref[...] + g_ref[0] * mix


def _outproj(oa, ob, w_out, x2, g1, l, seq, tm):
    n, d = x2.shape
    rows = lambda w: pl.BlockSpec((tm, w), lambda i: (i, 0))
    return pl.pallas_call(
        _outproj_kernel,
        grid=(n // tm,),
        in_specs=[
            rows(A_Q), rows(B_Q),
            pl.BlockSpec((None, A_Q + B_Q, d), lambda i: (l, 0, 0), pipeline_mode=pl.Buffered(1)),
            rows(d),
            pl.BlockSpec((1, 1, d), lambda i: ((i * tm) // seq, 0, 0)),
        ],
        out_specs=rows(d),
        out_shape=jax.ShapeDtypeStruct((n, d), f32),
        compiler_params=_cparams(("arbitrary",)),
        name="outproj",
    )(oa, ob, w_out, x2, g1)


def _ffn_kernel(x_ref, xh_ref, sc_ref, sh_ref, gate_ref, g_ref, wu_ref, wg_ref, cw_ref, cb_ref, wd_ref,
                o_ref, hbuf, acc, *, seq, tm):
    i = pl.program_id(0)
    j = pl.program_id(1)

    @pl.when(j == 0)
    def _():
        hbuf[HALO:, :] = _rms_mod(x_ref[...], g_ref[...], sc_ref[0], sh_ref[0]).astype(bf16)
        hbuf[:HALO, :] = _rms_mod(xh_ref[...], g_ref[...], sc_ref[0], sh_ref[0]).astype(bf16)
        acc[...] = jnp.zeros(acc.shape, f32)

    u = _dot(hbuf[...], wu_ref[...])
    gt = _dot(hbuf[HALO:, :], wg_ref[...])
    tpos = lax.rem(i * tm, seq) + lax.broadcasted_iota(jnp.int32, (tm, 1), 0)
    u1 = jnp.where(tpos >= 1, u[HALO - 1:HALO - 1 + tm], 0.0)
    u2 = jnp.where(tpos >= 2, u[HALO - 2:HALO - 2 + tm], 0.0)
    cw = cw_ref[...]
    a = cw[2:3] * u[HALO:] + cw[1:2] * u1 + cw[0:1] * u2 + cb_ref[...]
    act = (a * _sigmoid(a) * gt).astype(bf16)
    acc[...] = acc[...] + _dot(act, wd_ref[...])

    @pl.when(j == pl.num_programs(1) - 1)
    def _():
        o_ref[...] = x_ref[...] + gate_ref[0] * acc[...]


def _ffn(x2, sc, sh, gate, g, w_up, conv_w, conv_b, w_down, l, seq, tm, tf):
    n, d = x2.shape
    ff = w_down.shape[1]
    nf = ff // tf
    per_b = pl.BlockSpec((1, 1, d), lambda i, j: ((i * tm) // seq, 0, 0))
    halo_blocks = tm // HALO
    kern = functools.partial(_ffn_kernel, seq=seq, tm=tm)
    return pl.pallas_call(
        kern,
        grid=(n // tm, nf),
        in_specs=[
            pl.BlockSpec((tm, d), lambda i, j: (i, 0)),
            pl.BlockSpec((HALO, d), lambda i, j: (jnp.maximum(i * halo_blocks - 1, 0), 0)),
            per_b, per_b, per_b,
            pl.BlockSpec((None, 1, d), lambda i, j: (l, 0, 0)),
            pl.BlockSpec((None, d, tf), lambda i, j: (l, 0, j)),
            pl.BlockSpec((None, d, tf), lambda i, j: (l, 0, nf + j)),
            pl.BlockSpec((None, 3, tf), lambda i, j: (l, 0, j)),
            pl.BlockSpec((None, 1, tf), lambda i, j: (l, 0, j)),
            pl.BlockSpec((None, tf, d), lambda i, j: (l, j, 0)),
        ],
        out_specs=pl.BlockSpec((tm, d), lambda i, j: (i, 0)),
        out_shape=jax.ShapeDtypeStruct((n, d), f32),
        scratch_shapes=[pltpu.VMEM((tm + HALO, d), bf16), pltpu.VMEM((tm, d), f32)],
        compiler_params=_cparams(("arbitrary", "arbitrary")),
        name="ffn",
    )(x2, x2, sc, sh, gate, g, w_up, w_up, conv_w, conv_b, w_down)


def _final_norm_kernel(x_ref, g_ref, o_ref):
    x = x_ref[...]
    o_ref[...] = x * lax.rsqrt(jnp.mean(x * x, axis=-1, keepdims=True) + EPS) * g_ref[...]


def _final_norm(x2, g, tm):
    n, d = x2.shape
    return pl.pallas_call(
        _final_norm_kernel,
        grid=(n // tm,),
        in_specs=[pl.BlockSpec((tm, d), lambda i: (i, 0)), pl.BlockSpec((1, d), lambda i: (0, 0))],
        out_specs=pl.BlockSpec((tm, d), lambda i: (i, 0)),
        out_shape=jax.ShapeDtypeStruct((n, d), f32),
        compiler_params=_cparams(("arbitrary",)),
        name="final_norm",
    )(x2, g.reshape(1, d))


def _rel_bucket(dist):
    n = jnp.maximum(dist, 0)
    max_exact = NUM_BUCKETS // 2
    nf = jnp.maximum(n, 1).astype(f32)
    large = max_exact + (jnp.log(nf / max_exact) / math.log(MAX_DISTANCE / max_exact)
                         * (NUM_BUCKETS - max_exact)).astype(jnp.int32)
    large = jnp.minimum(large, NUM_BUCKETS - 1)
    return jnp.where(n < max_exact, n, large)


def _pack_w_in(w_in):
    o_ka = A_Q
    o_va = o_ka + A_KV
    o_qb = o_va + A_KV
    o_ckv = o_qb + B_Q
    o_qi = o_ckv + KV_RANK
    o_ki = o_qi + I_Q
    o_wi = o_ki + IDX_DIM
    sl = lambda a, b: w_in[:, :, a:b]
    dup = lambda a: [sl(a, a + HEAD_DIM)] * 2
    ki = sl(o_ki, o_wi)
    wi = sl(o_wi, o_wi + IDX_HEADS)
    pad = jnp.zeros(w_in.shape[:2] + (LANES - IDX_HEADS,), w_in.dtype)
    cols = ([sl(0, A_Q), sl(o_qb, o_ckv), sl(o_qi, o_ki), sl(o_ckv, o_qi)]
            + dup(o_ka) + dup(o_ka + HEAD_DIM) + dup(o_va) + dup(o_va + HEAD_DIM)
            + [ki, ki, wi, pad])
    return jnp.concatenate(cols, axis=-1).astype(bf16)


def _swa_tables(rel_a, sinks):
    t = jnp.arange(BLOCK, dtype=jnp.int32)[:, None]
    j = jnp.arange(2 * BLOCK, dtype=jnp.int32)[None, :]
    dist = t + BLOCK - j
    valid = (dist >= 0) & (dist < BLOCK)
    tab = jnp.where(valid[None], jnp.transpose(rel_a[_rel_bucket(dist)], (2, 0, 1)), NEG_INF)
    pairs = A_HEADS // A_KV_HEADS // 2
    order = [g * 2 * pairs + 2 * pr + par for g in range(A_KV_HEADS) for par in (0, 1) for pr in range(pairs)]
    tab = tab[jnp.array(order)].reshape(A_KV_HEADS, 2 * pairs * BLOCK, 2 * BLOCK).astype(f32)
    sink = jnp.repeat(sinks[:, jnp.array(order)], BLOCK, axis=1)
    return tab, sink.reshape(sinks.shape[0], A_KV_HEADS, 2 * pairs * BLOCK, 1).astype(f32)


def _dsa_tables(rel_b):
    t = jnp.arange(BLOCK, dtype=jnp.int32)[:, None]
    j = jnp.arange(BLOCK, dtype=jnp.int32)[None, :]
    far = rel_b[NUM_BUCKETS - 1][:, None, None]
    prev = jnp.transpose(rel_b[_rel_bucket(t + BLOCK - j)], (2, 0, 1)) - far
    diag = jnp.transpose(rel_b[_rel_bucket(t - j)], (2, 0, 1)) - far
    return jnp.stack([jnp.zeros_like(prev), prev, diag]).astype(f32)


def _pick(n, prefs):
    for p in prefs:
        if n % p == 0:
            return p
    return n


def kernel(x, c, w_mod, b_mod, norm_attn, w_in, attn_sinks, kv_norm, w_uk, w_uv, rel_bias, w_out,
           norm_ffn, w_up, conv_w, conv_b, w_down, norm_final):
    bsz, seq, d = x.shape
    depth = w_in.shape[0]
    ff = w_down.shape[1]
    n = bsz * seq
    assert seq % KCHUNK == 0 and d % LANES == 0 and ff % LANES == 0
    topk = min(TOPK_MAX, seq // 4)
    tm = _pick(seq, (512, 256, 128))
    tf = _pick(ff, (512, 256, 128))

    w_in_p = _pack_w_in(w_in)
    w_out_b = w_out.astype(bf16)
    w_up_b = w_up.astype(bf16)
    w_down_b = w_down.astype(bf16)
    wuk_p = jnp.swapaxes(w_uk, 2, 3).reshape(depth, B_HEADS // 2, 2 * HEAD_DIM, KV_RANK).astype(bf16)
    z = jnp.zeros_like(w_uv[:, 0::2])
    wuv_p = jnp.concatenate([jnp.concatenate([w_uv[:, 0::2], z], axis=-1),
                             jnp.concatenate([z, w_uv[:, 1::2]], axis=-1)], axis=2).astype(bf16)
    swa_bias, swa_sink = _swa_tables(rel_bias[:, :A_HEADS], attn_sinks)
    dsa_bias = _dsa_tables(rel_bias[:, A_HEADS:])
    norm_attn3 = norm_attn.reshape(depth, 1, d)
    norm_ffn3 = norm_ffn.reshape(depth, 1, d)
    kv_norm3 = kv_norm.reshape(depth, 1, KV_RANK)
    conv_b3 = conv_b.reshape(depth, 1, ff)

    mod = _modulation(c, w_mod, b_mod)
    x2 = x.reshape(n, d)
    for l in range(depth):
        sh1, sc1, g1, sh2, sc2, g2 = [mod[l, :, k * d:(k + 1) * d].reshape(bsz, 1, d) for k in range(6)]
        qa, qb, qi, ckv, ka2, va2, ki2, wi = _inproj(x2, sc1, sh1, norm_attn3, w_in_p, kv_norm3, l, seq, tm)
        r3 = lambda a: a.reshape(bsz, seq, a.shape[-1])
        out_a = _swa(r3(qa), r3(ka2), r3(va2), swa_bias, swa_sink[l])
        out_b = _dsa(r3(qb), r3(qi), r3(wi), r3(ki2), r3(ckv), wuk_p, wuv_p, dsa_bias, l, topk)
        x2 = _outproj(out_a.reshape(n, A_Q), out_b.reshape(n, B_Q), w_out_b, x2, g1, l, seq, tm)
        x2 = _ffn(x2, sc2, sh2, g2, norm_ffn3, w_up_b, conv_w, conv_b3, w_down_b, l, seq, tm, tf)
    return _final_norm(x2, norm_final, tm).reshape(bsz, seq, d)
```

```python
import functools
import math

import jax
import jax.numpy as jnp
import numpy as np
from jax import lax
from jax.experimental import pallas as pl
from jax.experimental.pallas import tpu as pltpu

HEAD_DIM = 64
A_HEADS = 16
A_KV_HEADS = 2
BLOCK = 128
B_HEADS = 16
KV_RANK = 256
IDX_HEADS = 16
IDX_DIM = 64
TOPK_MAX = 256
NUM_BUCKETS = 32
MAX_DISTANCE = 128
EPS = 1e-6
NEG_INF = -1e30
IDX_SCALE = IDX_DIM ** -0.5 * IDX_HEADS ** -0.5
QK_SCALE = HEAD_DIM ** -0.5
LOG2E = math.log2(math.e)

A_Q = A_HEADS * HEAD_DIM
A_KV = A_KV_HEADS * HEAD_DIM
B_Q = B_HEADS * HEAD_DIM
I_Q = IDX_HEADS * IDX_DIM

LANES = 128
KCHUNK = 256
HALO = 16
VMEM_LIMIT = 56 * 1024 * 1024

C_QA = 0
C_QB = C_QA + A_Q
C_QI = C_QB + B_Q
C_CKV = C_QI + I_Q
C_KA = C_CKV + KV_RANK
C_VA = C_KA + 2 * A_KV
C_KI = C_VA + 2 * A_KV
C_WI = C_KI + 2 * IDX_DIM
IN_PACKED = C_WI + LANES

bf16 = jnp.bfloat16
f32 = jnp.float32


def _cparams(sem):
    return pltpu.CompilerParams(dimension_semantics=sem, vmem_limit_bytes=VMEM_LIMIT)


def _dot(a, b):
    return jnp.dot(a, b, preferred_element_type=f32)


def _dot_nt(a, b):
    return lax.dot_general(a, b, (((1,), (1,)), ((), ())), preferred_element_type=f32)


def _sigmoid(x):
    return 1.0 / (1.0 + jnp.exp(-x))


def _order_key(bits):
    return bits ^ ((bits >> 31) & jnp.int32(0x7FFFFFFF))


_NEG_BITS = int(np.float32(NEG_INF).view(np.int32))
NEG_KEY = _NEG_BITS ^ ((_NEG_BITS >> 31) & 0x7FFFFFFF)


def _rms_mod(x, g, sc, sh):
    ms = jnp.mean(x * x, axis=-1, keepdims=True)
    y = x * lax.rsqrt(ms + EPS) * g
    return y * (1.0 + sc) + sh


def _mod_kernel(c_ref, w_ref, b_ref, o_ref):
    c = c_ref[...]
    ca = (c * _sigmoid(c)).astype(bf16)
    o_ref[...] = _dot(ca, w_ref[...].astype(bf16)) + b_ref[...]


def _modulation(c, w_mod, b_mod):
    depth, d, n6 = w_mod.shape
    bsz = c.shape[0]
    tn = 1536 if n6 % 1536 == 0 else n6
    return pl.pallas_call(
        _mod_kernel,
        grid=(depth, n6 // tn),
        in_specs=[
            pl.BlockSpec((bsz, d), lambda l, j: (0, 0)),
            pl.BlockSpec((None, d, tn), lambda l, j: (l, 0, j)),
            pl.BlockSpec((None, 1, tn), lambda l, j: (l, 0, j)),
        ],
        out_specs=pl.BlockSpec((None, bsz, tn), lambda l, j: (l, 0, j)),
        out_shape=jax.ShapeDtypeStruct((depth, bsz, n6), f32),
        compiler_params=_cparams(("arbitrary", "arbitrary")),
        name="modulation",
    )(c, w_mod, b_mod.reshape(depth, 1, n6))


def _inproj_kernel(x_ref, sc_ref, sh_ref, g_ref, w_ref, kvg_ref,
                   qa_ref, qb_ref, qi_ref, ckv_ref, ka_ref, va_ref, ki_ref, wi_ref):
    h = _rms_mod(x_ref[...], g_ref[...], sc_ref[0], sh_ref[0]).astype(bf16)

    def proj(c0, c1):
        return _dot(h, w_ref[:, c0:c1])

    qa_ref[...] = proj(C_QA, C_QB).astype(bf16)
    qb_ref[...] = proj(C_QB, C_QI).astype(bf16)
    qi_ref[...] = proj(C_QI, C_CKV).astype(bf16)
    ckv = proj(C_CKV, C_KA)
    ckv = ckv * lax.rsqrt(jnp.mean(ckv * ckv, axis=-1, keepdims=True) + EPS) * kvg_ref[...]
    ckv_ref[...] = ckv.astype(bf16)
    ka_ref[...] = proj(C_KA, C_VA).astype(bf16)
    va_ref[...] = proj(C_VA, C_KI).astype(bf16)
    ki_ref[...] = proj(C_KI, C_WI).astype(bf16)
    wi_ref[...] = proj(C_WI, IN_PACKED)


def _inproj(x2, sc, sh, g, w_packed, kvg, l, seq, tm):
    n, d = x2.shape
    rows = lambda w: pl.BlockSpec((tm, w), lambda i: (i, 0))
    per_b = pl.BlockSpec((1, 1, d), lambda i: ((i * tm) // seq, 0, 0))
    widths = (A_Q, B_Q, I_Q, KV_RANK, 2 * A_KV, 2 * A_KV, 2 * IDX_DIM, LANES)
    dtypes = (bf16,) * 7 + (f32,)
    return pl.pallas_call(
        _inproj_kernel,
        grid=(n // tm,),
        in_specs=[
            rows(d), per_b, per_b,
            pl.BlockSpec((None, 1, d), lambda i: (l, 0, 0)),
            pl.BlockSpec((None, d, IN_PACKED), lambda i: (l, 0, 0), pipeline_mode=pl.Buffered(1)),
            pl.BlockSpec((None, 1, KV_RANK), lambda i: (l, 0, 0)),
        ],
        out_specs=[rows(w) for w in widths],
        out_shape=[jax.ShapeDtypeStruct((n, w), dt) for w, dt in zip(widths, dtypes)],
        compiler_params=_cparams(("arbitrary",)),
        name="inproj",
    )(x2, sc, sh, g, w_packed, kvg)


def _swa_kernel(q_ref, kp_ref, kc_ref, vp_ref, vc_ref, bias_ref, sink_ref, o_ref):
    i = pl.program_id(1)
    q = q_ref[0]
    kk = jnp.concatenate([kp_ref[0], kc_ref[0]], axis=0)
    vv = jnp.concatenate([vp_ref[0], vc_ref[0]], axis=0)
    lo_q = lax.broadcasted_iota(jnp.int32, (BLOCK, LANES), 1) < HEAD_DIM
    lo_v = lax.broadcasted_iota(jnp.int32, (2 * BLOCK, LANES), 1) < HEAD_DIM
    col = lax.broadcasted_iota(jnp.int32, (1, 2 * BLOCK), 1)
    no_prev = jnp.logical_and(i == 0, col < BLOCK)
    zero = jnp.zeros((), bf16)
    pairs = A_HEADS // A_KV_HEADS // 2
    half = pairs * BLOCK
    for g in range(A_KV_HEADS):
        rows = []
        for par in (0, 1):
            for pr in range(pairs):
                qp = q[:, (g * pairs + pr) * LANES:(g * pairs + pr + 1) * LANES]
                rows.append(jnp.where(lo_q, qp, zero) if par == 0 else jnp.where(lo_q, zero, qp))
        qs = jnp.concatenate(rows, axis=0)
        s = _dot_nt(qs, kk[:, g * LANES:(g + 1) * LANES]) * QK_SCALE + bias_ref[g]
        s = jnp.where(no_prev, NEG_INF, s)
        sink = sink_ref[g]
        nrow = sink.shape[0]
        m = jnp.maximum(jnp.broadcast_to(jnp.max(s, axis=-1, keepdims=True), (nrow, LANES)), sink)
        e = jnp.exp(s - jnp.concatenate([m, m], axis=1))
        r = 1.0 / (jnp.broadcast_to(jnp.sum(e, axis=-1, keepdims=True), (nrow, LANES)) + jnp.exp(sink - m))
        eb = e.astype(bf16)
        v2 = vv[:, g * LANES:(g + 1) * LANES]
        o = (_dot(eb[:half], jnp.where(lo_v, v2, zero)) * r[:half]
             + _dot(eb[half:], jnp.where(lo_v, zero, v2)) * r[half:])
        for pr in range(pairs):
            c0 = (g * pairs + pr) * LANES
            o_ref[0, :, c0:c0 + LANES] = o[pr * BLOCK:(pr + 1) * BLOCK].astype(bf16)


def _swa(qa, ka2, va2, bias, sink):
    bsz, seq, _ = qa.shape
    nblk = seq // BLOCK
    cur = lambda w: pl.BlockSpec((1, BLOCK, w), lambda b, i: (b, i, 0))
    prev = lambda w: pl.BlockSpec((1, BLOCK, w), lambda b, i: (b, jnp.maximum(i - 1, 0), 0))
    return pl.pallas_call(
        _swa_kernel,
        grid=(bsz, nblk),
        in_specs=[
            cur(A_Q), prev(2 * A_KV), cur(2 * A_KV), prev(2 * A_KV), cur(2 * A_KV),
            pl.BlockSpec(bias.shape, lambda b, i: (0, 0, 0)),
            pl.BlockSpec(sink.shape, lambda b, i: (0, 0, 0)),
        ],
        out_specs=cur(A_Q),
        out_shape=jax.ShapeDtypeStruct((bsz, seq, A_Q), bf16),
        compiler_params=_cparams(("arbitrary", "arbitrary")),
        name="swa",
    )(qa, ka2, ka2, va2, va2, bias, sink)


def _dsa_kernel(qb_ref, qi_ref, wi_ref, ki_ref, ckv_ref, wuk_ref, wuv_ref, bias_ref, o_ref,
                qstk, qlat, wb, keys, keys_t, s_buf, p_buf, acc, mrun, lrun, *, topk, nch_total):
    i = pl.program_id(1)
    nch = i // 2 + 1
    rows_all = B_HEADS * BLOCK
    lo = lax.broadcasted_iota(jnp.int32, (BLOCK, LANES), 1) < HEAD_DIM
    zero = jnp.zeros((), bf16)

    qi = qi_ref[0]
    qb = qb_ref[0]
    for pr in range(B_HEADS // 2):
        qp = qi[:, pr * LANES:(pr + 1) * LANES]
        qstk[(2 * pr) * BLOCK:(2 * pr + 1) * BLOCK, :] = jnp.where(lo, qp, zero)
        qstk[(2 * pr + 1) * BLOCK:(2 * pr + 2) * BLOCK, :] = jnp.where(lo, zero, qp)
        qp = qb[:, pr * LANES:(pr + 1) * LANES]
        w = wuk_ref[pr]
        qlat[(2 * pr) * BLOCK:(2 * pr + 1) * BLOCK, :] = (
            _dot(jnp.where(lo, qp, zero), w) * (QK_SCALE * LOG2E)).astype(bf16)
        qlat[(2 * pr + 1) * BLOCK:(2 * pr + 2) * BLOCK, :] = (
            _dot(jnp.where(lo, zero, qp), w) * (QK_SCALE * LOG2E)).astype(bf16)
    wi = wi_ref[0]
    for h in range(IDX_HEADS):
        wb[h] = jnp.broadcast_to(wi[:, h:h + 1] * IDX_SCALE, (BLOCK, KCHUNK))

    qpos = i * BLOCK + lax.broadcasted_iota(jnp.int32, (BLOCK, KCHUNK), 0)
    kiota = lax.broadcasted_iota(jnp.int32, (BLOCK, KCHUNK), 1)

    def score_body(c, carry):
        start = pl.multiple_of(c * KCHUNK, KCHUNK)
        d = _dot_nt(qstk[...], ki_ref[0, pl.ds(start, KCHUNK), :])
        s_buf[c] = _dot_nt(qlat[...], ckv_ref[0, pl.ds(start, KCHUNK), :])
        sc = jnp.zeros((BLOCK, KCHUNK), f32)
        for h in range(IDX_HEADS):
            sc = sc + wb[h] * jnp.maximum(d[h * BLOCK:(h + 1) * BLOCK], 0.0)
        sc = jnp.where(start + kiota <= qpos, sc, NEG_INF)
        keys[c] = _order_key(pltpu.bitcast(sc, jnp.int32))
        keys_t[c] = _order_key(pltpu.bitcast(sc.T, jnp.int32))
        return carry

    lax.fori_loop(0, nch, score_body, 0)

    rest = ((nch_total - nch) * KCHUNK).astype(f32)
    part = 32

    def count_ge(cand):
        def body(c, cnt):
            ind = jnp.where(keys_t[c] >= cand, 1.0, 0.0)
            return cnt + jnp.sum(ind.reshape(KCHUNK // part, part, BLOCK), axis=0)
        cnt = lax.fori_loop(0, nch, body, jnp.zeros((part, BLOCK), f32))
        tot = jnp.sum(cnt, axis=0, keepdims=True)
        return tot + jnp.where(cand <= NEG_KEY, rest, 0.0)

    kf = jnp.float32(topk)
    tau0 = jnp.where(count_ge(jnp.zeros((1, BLOCK), jnp.int32)) >= kf,
                     jnp.int32(0), jnp.int32(-2 ** 31))

    def bisect_body(t, tau):
        cand = tau + lax.shift_left(jnp.int32(1), 30 - t)
        return jnp.where(count_ge(cand) >= kf, cand, tau)

    tau = lax.fori_loop(0, 31, bisect_body, tau0)
    tau_rows = jnp.broadcast_to(tau, (BLOCK, BLOCK)).T
    tau_rows = jnp.concatenate([tau_rows, tau_rows], axis=1)

    mrun[...] = jnp.full(mrun.shape, NEG_INF, f32)
    nfar = jnp.maximum(nch - 2, 0)

    def pass1(c, near):
        start = pl.multiple_of(c * KCHUNK, KCHUNK)
        sel = keys[c] >= tau_rows
        if near:
            sel = jnp.logical_and(sel, start + kiota <= qpos)
            t0 = jnp.clip(2 * c - i + 2, 0, 2)
            t1 = jnp.clip(2 * c + 1 - i + 2, 0, 2)
        mb = jnp.where(sel, 0.0, NEG_INF)
        for h in range(B_HEADS):
            r0, r1 = h * BLOCK, (h + 1) * BLOCK
            s = s_buf[c, r0:r1, :] + mb
            if near:
                s = s + jnp.concatenate([bias_ref[t0, h], bias_ref[t1, h]], axis=1)
            s_buf[c, r0:r1, :] = s
            mrun[r0:r1, :] = jnp.maximum(mrun[r0:r1, :], jnp.maximum(s[:, :LANES], s[:, LANES:]))

    def pass1_far(c, carry):
        pass1(c, False)
        return carry

    def pass1_near(c, carry):
        pass1(c, True)
        return carry

    lax.fori_loop(0, nfar, pass1_far, 0)
    lax.fori_loop(nfar, nch, pass1_near, 0)

    for h in range(B_HEADS):
        r0, r1 = h * BLOCK, (h + 1) * BLOCK
        mrun[r0:r1, :] = jnp.broadcast_to(jnp.max(mrun[r0:r1, :], axis=-1, keepdims=True), (BLOCK, LANES))

    acc[...] = jnp.zeros(acc.shape, f32)
    lrun[...] = jnp.zeros(lrun.shape, f32)

    def pass2(c, carry):
        start = pl.multiple_of(c * KCHUNK, KCHUNK)
        for h in range(B_HEADS):
            r0, r1 = h * BLOCK, (h + 1) * BLOCK
            m = mrun[r0:r1, :]
            p = jnp.exp2(s_buf[c, r0:r1, :] - jnp.concatenate([m, m], axis=1))
            lrun[r0:r1, :] = lrun[r0:r1, :] + p[:, :LANES] + p[:, LANES:]
            p_buf[r0:r1, :] = p.astype(bf16)
        acc[...] = acc[...] + _dot(p_buf[...], ckv_ref[0, pl.ds(start, KCHUNK), :])
        return carry

    lax.fori_loop(0, nch, pass2, 0)

    for pr in range(B_HEADS // 2):
        parts = []
        for par in (0, 1):
            r0 = (2 * pr + par) * BLOCK
            l = jnp.sum(lrun[r0:r0 + BLOCK, :], axis=-1, keepdims=True)
            parts.append((acc[r0:r0 + BLOCK, :] * (1.0 / l)).astype(bf16))
        o = _dot(jnp.concatenate(parts, axis=1), wuv_ref[pr])
        o_ref[0, :, pr * LANES:(pr + 1) * LANES] = o.astype(bf16)


def _dsa(qb, qi, wi, ki2, ckv, wuk_p, wuv_p, bias_t, l, topk):
    bsz, seq, _ = qb.shape
    nblk = seq // BLOCK
    nch_total = seq // KCHUNK
    rows_all = B_HEADS * BLOCK
    cur = lambda w: pl.BlockSpec((1, BLOCK, w), lambda b, i: (b, i, 0))
    whole = lambda w: pl.BlockSpec((1, seq, w), lambda b, i: (b, 0, 0))
    kern = functools.partial(_dsa_kernel, topk=topk, nch_total=nch_total)
    return pl.pallas_call(
        kern,
        grid=(bsz, nblk),
        in_specs=[
            cur(B_Q), cur(I_Q), cur(LANES), whole(2 * IDX_DIM), whole(KV_RANK),
            pl.BlockSpec((None,) + wuk_p.shape[1:], lambda b, i: (l, 0, 0, 0)),
            pl.BlockSpec((None,) + wuv_p.shape[1:], lambda b, i: (l, 0, 0, 0)),
            pl.BlockSpec(bias_t.shape, lambda b, i: (0, 0, 0, 0)),
        ],
        out_specs=cur(B_Q),
        out_shape=jax.ShapeDtypeStruct((bsz, seq, B_Q), bf16),
        scratch_shapes=[
            pltpu.VMEM((rows_all, LANES), bf16),
            pltpu.VMEM((rows_all, KV_RANK), bf16),
            pltpu.VMEM((IDX_HEADS, BLOCK, KCHUNK), f32),
            pltpu.VMEM((nch_total, BLOCK, KCHUNK), jnp.int32),
            pltpu.VMEM((nch_total, KCHUNK, BLOCK), jnp.int32),
            pltpu.VMEM((nch_total, rows_all, KCHUNK), f32),
            pltpu.VMEM((rows_all, KCHUNK), bf16),
            pltpu.VMEM((rows_all, KV_RANK), f32),
            pltpu.VMEM((rows_all, LANES), f32),
            pltpu.VMEM((rows_all, LANES), f32),
        ],
        compiler_params=_cparams(("arbitrary", "arbitrary")),
        name="dsa",
    )(qb, qi, wi, ki2, ckv, wuk_p, wuv_p, bias_t)


def _outproj_kernel(oa_ref, ob_ref, w_ref, x_ref, g_ref, o_ref):
    mix = _dot(oa_ref[...], w_ref[:A_Q, :]) + _dot(ob_ref[...], w_ref[A_Q:, :])
    o_ref[...] = x_ref[...] + g_ref[0] * mix


def _outproj(oa, ob, w_out, x2, g1, l, seq, tm):
    n, d = x2.shape
    rows = lambda w: pl.BlockSpec((tm, w), lambda i: (i, 0))
    return pl.pallas_call(
        _outproj_kernel,
        grid=(n // tm,),
        in_specs=[
            rows(A_Q), rows(B_Q),
            pl.BlockSpec((None, A_Q + B_Q, d), lambda i: (l, 0, 0), pipeline_mode=pl.Buffered(1)),
            rows(d),
            pl.BlockSpec((1, 1, d), lambda i: ((i * tm) // seq, 0, 0)),
        ],
        out_specs=rows(d),
        out_shape=jax.ShapeDtypeStruct((n, d), f32),
        compiler_params=_cparams(("arbitrary",)),
        name="outproj",
    )(oa, ob, w_out, x2, g1)


def _ffn_kernel(x_ref, xh_ref, sc_ref, sh_ref, gate_ref, g_ref, wu_ref, wg_ref, cw_ref, cb_ref, wd_ref,
                o_ref, hbuf, acc, *, seq, tm):
    i = pl.program_id(0)
    j = pl.program_id(1)

    @pl.when(j == 0)
    def _():
        hbuf[HALO:, :] = _rms_mod(x_ref[...], g_ref[...], sc_ref[0], sh_ref[0]).astype(bf16)
        hbuf[:HALO, :] = _rms_mod(xh_ref[...], g_ref[...], sc_ref[0], sh_ref[0]).astype(bf16)
        acc[...] = jnp.zeros(acc.shape, f32)

    u = _dot(hbuf[...], wu_ref[...])
    gt = _dot(hbuf[HALO:, :], wg_ref[...])
    tpos = lax.rem(i * tm, seq) + lax.broadcasted_iota(jnp.int32, (tm, 1), 0)
    u1 = jnp.where(tpos >= 1, u[HALO - 1:HALO - 1 + tm], 0.0)
    u2 = jnp.where(tpos >= 2, u[HALO - 2:HALO - 2 + tm], 0.0)
    cw = cw_ref[...]
    a = cw[2:3] * u[HALO:] + cw[1:2] * u1 + cw[0:1] * u2 + cb_ref[...]
    act = (a * _sigmoid(a) * gt).astype(bf16)
    acc[...] = acc[...] + _dot(act, wd_ref[...])

    @pl.when(j == pl.num_programs(1) - 1)
    def _():
        o_ref[...] = x_ref[...] + gate_ref[0] * acc[...]


def _ffn(x2, sc, sh, gate, g, w_up, conv_w, conv_b, w_down, l, seq, tm, tf):
    n, d = x2.shape
    ff = w_down.shape[1]
    nf = ff // tf
    per_b = pl.BlockSpec((1, 1, d), lambda i, j: ((i * tm) // seq, 0, 0))
    halo_blocks = tm // HALO
    kern = functools.partial(_ffn_kernel, seq=seq, tm=tm)
    return pl.pallas_call(
        kern,
        grid=(n // tm, nf),
        in_specs=[
            pl.BlockSpec((tm, d), lambda i, j: (i, 0)),
            pl.BlockSpec((HALO, d), lambda i, j: (jnp.maximum(i * halo_blocks - 1, 0), 0)),
            per_b, per_b, per_b,
            pl.BlockSpec((None, 1, d), lambda i, j: (l, 0, 0)),
            pl.BlockSpec((None, d, tf), lambda i, j: (l, 0, j)),
            pl.BlockSpec((None, d, tf), lambda i, j: (l, 0, nf + j)),
            pl.BlockSpec((None, 3, tf), lambda i, j: (l, 0, j)),
            pl.BlockSpec((None, 1, tf), lambda i, j: (l, 0, j)),
            pl.BlockSpec((None, tf, d), lambda i, j: (l, j, 0)),
        ],
        out_specs=pl.BlockSpec((tm, d), lambda i, j: (i, 0)),
        out_shape=jax.ShapeDtypeStruct((n, d), f32),
        scratch_shapes=[pltpu.VMEM((tm + HALO, d), bf16), pltpu.VMEM((tm, d), f32)],
        compiler_params=_cparams(("arbitrary", "arbitrary")),
        name="ffn",
    )(x2, x2, sc, sh, gate, g, w_up, w_up, conv_w, conv_b, w_down)


def _final_norm_kernel(x_ref, g_ref, o_ref):
    x = x_ref[...]
    o_ref[...] = x * lax.rsqrt(jnp.mean(x * x, axis=-1, keepdims=True) + EPS) * g_ref[...]


def _final_norm(x2, g, tm):
    n, d = x2.shape
    return pl.pallas_call(
        _final_norm_kernel,
        grid=(n // tm,),
        in_specs=[pl.BlockSpec((tm, d), lambda i: (i, 0)), pl.BlockSpec((1, d), lambda i: (0, 0))],
        out_specs=pl.BlockSpec((tm, d), lambda i: (i, 0)),
        out_shape=jax.ShapeDtypeStruct((n, d), f32),
        compiler_params=_cparams(("arbitrary",)),
        name="final_norm",
    )(x2, g.reshape(1, d))


def _rel_bucket(dist):
    n = jnp.maximum(dist, 0)
    max_exact = NUM_BUCKETS // 2
    nf = jnp.maximum(n, 1).astype(f32)
    large = max_exact + (jnp.log(nf / max_exact) / math.log(MAX_DISTANCE / max_exact)
                         * (NUM_BUCKETS - max_exact)).astype(jnp.int32)
    large = jnp.minimum(large, NUM_BUCKETS - 1)
    return jnp.where(n < max_exact, n, large)


def _pack_w_in(w_in):
    o_ka = A_Q
    o_va = o_ka + A_KV
    o_qb = o_va + A_KV
    o_ckv = o_qb + B_Q
    o_qi = o_ckv + KV_RANK
    o_ki = o_qi + I_Q
    o_wi = o_ki + IDX_DIM
    sl = lambda a, b: w_in[:, :, a:b]
    dup = lambda a: [sl(a, a + HEAD_DIM)] * 2
    ki = sl(o_ki, o_wi)
    wi = sl(o_wi, o_wi + IDX_HEADS)
    pad = jnp.zeros(w_in.shape[:2] + (LANES - IDX_HEADS,), w_in.dtype)
    cols = ([sl(0, A_Q), sl(o_qb, o_ckv), sl(o_qi, o_ki), sl(o_ckv, o_qi)]
            + dup(o_ka) + dup(o_ka + HEAD_DIM) + dup(o_va) + dup(o_va + HEAD_DIM)
            + [ki, ki, wi, pad])
    return jnp.concatenate(cols, axis=-1).astype(bf16)


def _swa_tables(rel_a, sinks):
    t = jnp.arange(BLOCK, dtype=jnp.int32)[:, None]
    j = jnp.arange(2 * BLOCK, dtype=jnp.int32)[None, :]
    dist = t + BLOCK - j
    valid = (dist >= 0) & (dist < BLOCK)
    tab = jnp.where(valid[None], jnp.transpose(rel_a[_rel_bucket(dist)], (2, 0, 1)), NEG_INF)
    pairs = A_HEADS // A_KV_HEADS // 2
    order = [g * 2 * pairs + 2 * pr + par for g in range(A_KV_HEADS) for par in (0, 1) for pr in range(pairs)]
    tab = tab[jnp.array(order)].reshape(A_KV_HEADS, 2 * pairs * BLOCK, 2 * BLOCK).astype(f32)
    sink = jnp.repeat(sinks[:, jnp.array(order)], BLOCK, axis=1).reshape(-1, A_KV_HEADS, 2 * pairs * BLOCK, 1)
    return tab, jnp.broadcast_to(sink, sink.shape[:-1] + (LANES,)).astype(f32)


def _dsa_tables(rel_b):
    t = jnp.arange(BLOCK, dtype=jnp.int32)[:, None]
    j = jnp.arange(BLOCK, dtype=jnp.int32)[None, :]
    far = rel_b[NUM_BUCKETS - 1][:, None, None]
    prev = jnp.transpose(rel_b[_rel_bucket(t + BLOCK - j)], (2, 0, 1)) - far
    diag = jnp.transpose(rel_b[_rel_bucket(t - j)], (2, 0, 1)) - far
    return (jnp.stack([jnp.zeros_like(prev), prev, diag]) * LOG2E).astype(f32)


def _pick(n, prefs):
    for p in prefs:
        if n % p == 0:
            return p
    return n


def kernel(x, c, w_mod, b_mod, norm_attn, w_in, attn_sinks, kv_norm, w_uk, w_uv, rel_bias, w_out,
           norm_ffn, w_up, conv_w, conv_b, w_down, norm_final):
    bsz, seq, d = x.shape
    depth = w_in.shape[0]
    ff = w_down.shape[1]
    n = bsz * seq
    assert seq % KCHUNK == 0 and d % LANES == 0 and ff % LANES == 0
    topk = min(TOPK_MAX, seq // 4)
    tm = _pick(seq, (512, 256, 128))
    tf = _pick(ff, (512, 256, 128))

    w_in_p = _pack_w_in(w_in)
    w_out_b = w_out.astype(bf16)
    w_up_b = w_up.astype(bf16)
    w_down_b = w_down.astype(bf16)
    wuk_p = jnp.swapaxes(w_uk, 2, 3).reshape(depth, B_HEADS // 2, 2 * HEAD_DIM, KV_RANK).astype(bf16)
    z = jnp.zeros_like(w_uv[:, 0::2])
    wuv_p = jnp.concatenate([jnp.concatenate([w_uv[:, 0::2], z], axis=-1),
                             jnp.concatenate([z, w_uv[:, 1::2]], axis=-1)], axis=2).astype(bf16)
    swa_bias, swa_sink = _swa_tables(rel_bias[:, :A_HEADS], attn_sinks)
    dsa_bias = _dsa_tables(rel_bias[:, A_HEADS:])
    norm_attn3 = norm_attn.reshape(depth, 1, d)
    norm_ffn3 = norm_ffn.reshape(depth, 1, d)
    kv_norm3 = kv_norm.reshape(depth, 1, KV_RANK)
    conv_b3 = conv_b.reshape(depth, 1, ff)

    mod = _modulation(c, w_mod, b_mod)
    x2 = x.reshape(n, d)
    for l in range(depth):
        sh1, sc1, g1, sh2, sc2, g2 = [mod[l, :, k * d:(k + 1) * d].reshape(bsz, 1, d) for k in range(6)]
        qa, qb, qi, ckv, ka2, va2, ki2, wi = _inproj(x2, sc1, sh1, norm_attn3, w_in_p, kv_norm3, l, seq, tm)
        r3 = lambda a: a.reshape(bsz, seq, a.shape[-1])
        out_a = _swa(r3(qa), r3(ka2), r3(va2), swa_bias, swa_sink[l])
        out_b = _dsa(r3(qb), r3(qi), r3(wi), r3(ki2), r3(ckv), wuk_p, wuv_p, dsa_bias, l, topk)
        x2 = _outproj(out_a.reshape(n, A_Q), out_b.reshape(n, B_Q), w_out_b, x2, g1, l, seq, tm)
        x2 = _ffn(x2, sc2, sh2, g2, norm_ffn3, w_up_b, conv_w, conv_b3, w_down_b, l, seq, tm, tf)
    return _final_norm(x2, norm_final, tm).reshape(bsz, seq, d)
```

```python
import functools
import math

import jax
import jax.numpy as jnp
import numpy as np
from jax import lax
from jax.experimental import pallas as pl
from jax.experimental.pallas import tpu as pltpu

HEAD_DIM = 64
A_HEADS = 16
A_KV_HEADS = 2
BLOCK = 128
B_HEADS = 16
KV_RANK = 256
IDX_HEADS = 16
IDX_DIM = 64
TOPK_MAX = 256
NUM_BUCKETS = 32
MAX_DISTANCE = 128
EPS = 1e-6
NEG_INF = -1e30
IDX_SCALE = IDX_DIM ** -0.5 * IDX_HEADS ** -0.5
QK_SCALE = HEAD_DIM ** -0.5
LOG2E = math.log2(math.e)

A_Q = A_HEADS * HEAD_DIM
A_KV = A_KV_HEADS * HEAD_DIM
B_Q = B_HEADS * HEAD_DIM
I_Q = IDX_HEADS * IDX_DIM

LANES = 128
KCHUNK = 256
HALO = 16
VMEM_LIMIT = 56 * 1024 * 1024

C_QA = 0
C_QB = C_QA + A_Q
C_QI = C_QB + B_Q
C_CKV = C_QI + I_Q
C_KA = C_CKV + KV_RANK
C_VA = C_KA + 2 * A_KV
C_KI = C_VA + 2 * A_KV
C_WI = C_KI + 2 * IDX_DIM
IN_PACKED = C_WI + LANES

bf16 = jnp.bfloat16
f32 = jnp.float32


def _cparams(sem):
    return pltpu.CompilerParams(dimension_semantics=sem, vmem_limit_bytes=VMEM_LIMIT)


def _dot(a, b):
    return jnp.dot(a, b, preferred_element_type=f32)


def _dot_nt(a, b):
    return lax.dot_general(a, b, (((1,), (1,)), ((), ())), preferred_element_type=f32)


def _sigmoid(x):
    return 1.0 / (1.0 + jnp.exp(-x))


def _order_key(bits):
    return bits ^ ((bits >> 31) & jnp.int32(0x7FFFFFFF))


_NEG_BITS = int(np.float32(NEG_INF).view(np.int32))
NEG_KEY = _NEG_BITS ^ ((_NEG_BITS >> 31) & 0x7FFFFFFF)


def _rms_mod(x, g, sc, sh):
    ms = jnp.mean(x * x, axis=-1, keepdims=True)
    y = x * lax.rsqrt(ms + EPS) * g
    return y * (1.0 + sc) + sh


def _mod_kernel(c_ref, w_ref, b_ref, o_ref):
    c = c_ref[...]
    ca = (c * _sigmoid(c)).astype(bf16)
    o_ref[...] = _dot(ca, w_ref[...].astype(bf16)) + b_ref[...]


def _modulation(c, w_mod, b_mod):
    depth, d, n6 = w_mod.shape
    bsz = c.shape[0]
    tn = 1536 if n6 % 1536 == 0 else n6
    return pl.pallas_call(
        _mod_kernel,
        grid=(depth, n6 // tn),
        in_specs=[
            pl.BlockSpec((bsz, d), lambda l, j: (0, 0)),
            pl.BlockSpec((None, d, tn), lambda l, j: (l, 0, j)),
            pl.BlockSpec((None, 1, tn), lambda l, j: (l, 0, j)),
        ],
        out_specs=pl.BlockSpec((None, bsz, tn), lambda l, j: (l, 0, j)),
        out_shape=jax.ShapeDtypeStruct((depth, bsz, n6), f32),
        compiler_params=_cparams(("arbitrary", "arbitrary")),
        name="modulation",
    )(c, w_mod, b_mod.reshape(depth, 1, n6))


def _inproj_kernel(x_ref, sc_ref, sh_ref, g_ref, w_ref, kvg_ref,
                   qa_ref, qb_ref, qi_ref, ckv_ref, ka_ref, va_ref, ki_ref, wi_ref):
    h = _rms_mod(x_ref[...], g_ref[...], sc_ref[0], sh_ref[0]).astype(bf16)

    def proj(c0, c1):
        return _dot(h, w_ref[:, c0:c1])

    qa_ref[...] = proj(C_QA, C_QB).astype(bf16)
    qb_ref[...] = proj(C_QB, C_QI).astype(bf16)
    qi_ref[...] = proj(C_QI, C_CKV).astype(bf16)
    ckv = proj(C_CKV, C_KA)
    ckv = ckv * lax.rsqrt(jnp.mean(ckv * ckv, axis=-1, keepdims=True) + EPS) * kvg_ref[...]
    ckv_ref[...] = ckv.astype(bf16)
    ka_ref[...] = proj(C_KA, C_VA).astype(bf16)
    va_ref[...] = proj(C_VA, C_KI).astype(bf16)
    ki_ref[...] = proj(C_KI, C_WI).astype(bf16)
    wi_ref[...] = proj(C_WI, IN_PACKED)


def _inproj(x2, sc, sh, g, w_packed, kvg, l, seq, tm):
    n, d = x2.shape
    rows = lambda w: pl.BlockSpec((tm, w), lambda i: (i, 0))
    per_b = pl.BlockSpec((1, 1, d), lambda i: ((i * tm) // seq, 0, 0))
    widths = (A_Q, B_Q, I_Q, KV_RANK, 2 * A_KV, 2 * A_KV, 2 * IDX_DIM, LANES)
    dtypes = (bf16,) * 7 + (f32,)
    return pl.pallas_call(
        _inproj_kernel,
        grid=(n // tm,),
        in_specs=[
            rows(d), per_b, per_b,
            pl.BlockSpec((None, 1, d), lambda i: (l, 0, 0)),
            pl.BlockSpec((None, d, IN_PACKED), lambda i: (l, 0, 0), pipeline_mode=pl.Buffered(1)),
            pl.BlockSpec((None, 1, KV_RANK), lambda i: (l, 0, 0)),
        ],
        out_specs=[rows(w) for w in widths],
        out_shape=[jax.ShapeDtypeStruct((n, w), dt) for w, dt in zip(widths, dtypes)],
        compiler_params=_cparams(("arbitrary",)),
        name="inproj",
    )(x2, sc, sh, g, w_packed, kvg)


def _swa_kernel(q_ref, kp_ref, kc_ref, vp_ref, vc_ref, bias_ref, sink_ref, o_ref):
    i = pl.program_id(1)
    q = q_ref[0]
    kk = jnp.concatenate([kp_ref[0], kc_ref[0]], axis=0)
    vv = jnp.concatenate([vp_ref[0], vc_ref[0]], axis=0)
    lo_q = lax.broadcasted_iota(jnp.int32, (BLOCK, LANES), 1) < HEAD_DIM
    lo_v = lax.broadcasted_iota(jnp.int32, (2 * BLOCK, LANES), 1) < HEAD_DIM
    col = lax.broadcasted_iota(jnp.int32, (1, 2 * BLOCK), 1)
    no_prev = jnp.logical_and(i == 0, col < BLOCK)
    zero = jnp.zeros((), bf16)
    pairs = A_HEADS // A_KV_HEADS // 2
    half = pairs * BLOCK
    for g in range(A_KV_HEADS):
        rows = []
        for par in (0, 1):
            for pr in range(pairs):
                qp = q[:, (g * pairs + pr) * LANES:(g * pairs + pr + 1) * LANES]
                rows.append(jnp.where(lo_q, qp, zero) if par == 0 else jnp.where(lo_q, zero, qp))
        qs = jnp.concatenate(rows, axis=0)
        s = _dot_nt(qs, kk[:, g * LANES:(g + 1) * LANES]) * QK_SCALE + bias_ref[g]
        s = jnp.where(no_prev, NEG_INF, s)
        sink = sink_ref[g]
        nrow = sink.shape[0]
        m = jnp.maximum(jnp.broadcast_to(jnp.max(s, axis=-1, keepdims=True), (nrow, LANES)), sink)
        e = jnp.exp(s - jnp.concatenate([m, m], axis=1))
        r = 1.0 / (jnp.broadcast_to(jnp.sum(e, axis=-1, keepdims=True), (nrow, LANES)) + jnp.exp(sink - m))
        eb = e.astype(bf16)
        v2 = vv[:, g * LANES:(g + 1) * LANES]
        o = (_dot(eb[:half], jnp.where(lo_v, v2, zero)) * r[:half]
             + _dot(eb[half:], jnp.where(lo_v, zero, v2)) * r[half:])
        for pr in range(pairs):
            c0 = (g * pairs + pr) * LANES
            o_ref[0, :, c0:c0 + LANES] = o[pr * BLOCK:(pr + 1) * BLOCK].astype(bf16)


def _swa(qa, ka2, va2, bias, sink):
    bsz, seq, _ = qa.shape
    nblk = seq // BLOCK
    cur = lambda w: pl.BlockSpec((1, BLOCK, w), lambda b, i: (b, i, 0))
    prev = lambda w: pl.BlockSpec((1, BLOCK, w), lambda b, i: (b, jnp.maximum(i - 1, 0), 0))
    return pl.pallas_call(
        _swa_kernel,
        grid=(bsz, nblk),
        in_specs=[
            cur(A_Q), prev(2 * A_KV), cur(2 * A_KV), prev(2 * A_KV), cur(2 * A_KV),
            pl.BlockSpec(bias.shape, lambda b, i: (0, 0, 0)),
            pl.BlockSpec(sink.shape, lambda b, i: (0, 0, 0)),
        ],
        out_specs=cur(A_Q),
        out_shape=jax.ShapeDtypeStruct((bsz, seq, A_Q), bf16),
        compiler_params=_cparams(("arbitrary", "arbitrary")),
        name="swa",
    )(qa, ka2, ka2, va2, va2, bias, sink)


def _chunk_pairs(body, lo, hi):
    n = hi - lo

    def two(k, carry):
        body(lo + 2 * k)
        body(lo + 2 * k + 1)
        return carry

    lax.fori_loop(0, lax.shift_right_logical(n, 1), two, 0)

    @pl.when((n & 1) == 1)
    def _():
        body(hi - 1)


def _dsa_kernel(qb_ref, qi_ref, wi_ref, ki_ref, ckv_ref, wuk_ref, wuv_ref, bias_ref, o_ref,
                qstk, qlat, wb, keys, keys_t, s_buf, p_buf, acc, mrun, lrun, *, topk, nch_total):
    i = pl.program_id(1)
    nch = i // 2 + 1
    rows_all = B_HEADS * BLOCK
    lo = lax.broadcasted_iota(jnp.int32, (BLOCK, LANES), 1) < HEAD_DIM
    zero = jnp.zeros((), bf16)

    qi = qi_ref[0]
    qb = qb_ref[0]
    for pr in range(B_HEADS // 2):
        qp = qi[:, pr * LANES:(pr + 1) * LANES]
        qstk[(2 * pr) * BLOCK:(2 * pr + 1) * BLOCK, :] = jnp.where(lo, qp, zero)
        qstk[(2 * pr + 1) * BLOCK:(2 * pr + 2) * BLOCK, :] = jnp.where(lo, zero, qp)
        qp = qb[:, pr * LANES:(pr + 1) * LANES]
        w = wuk_ref[pr]
        qlat[(2 * pr) * BLOCK:(2 * pr + 1) * BLOCK, :] = (
            _dot(jnp.where(lo, qp, zero), w) * (QK_SCALE * LOG2E)).astype(bf16)
        qlat[(2 * pr + 1) * BLOCK:(2 * pr + 2) * BLOCK, :] = (
            _dot(jnp.where(lo, zero, qp), w) * (QK_SCALE * LOG2E)).astype(bf16)
    wi = wi_ref[0]
    for h in range(IDX_HEADS):
        wb[h] = jnp.broadcast_to(wi[:, h:h + 1] * IDX_SCALE, (BLOCK, KCHUNK))

    qpos = i * BLOCK + lax.broadcasted_iota(jnp.int32, (BLOCK, KCHUNK), 0)
    kiota = lax.broadcasted_iota(jnp.int32, (BLOCK, KCHUNK), 1)

    def score_body(c):
        start = pl.multiple_of(c * KCHUNK, KCHUNK)
        d = _dot_nt(qstk[...], ki_ref[0, pl.ds(start, KCHUNK), :])
        s_buf[c] = _dot_nt(qlat[...], ckv_ref[0, pl.ds(start, KCHUNK), :])
        sc = jnp.zeros((BLOCK, KCHUNK), f32)
        for h in range(IDX_HEADS):
            sc = sc + wb[h] * jnp.maximum(d[h * BLOCK:(h + 1) * BLOCK], 0.0)
        sc = jnp.where(start + kiota <= qpos, sc, NEG_INF)
        keys[c] = _order_key(pltpu.bitcast(sc, jnp.int32))
        keys_t[c] = _order_key(pltpu.bitcast(sc.T, jnp.int32))

    _chunk_pairs(score_body, 0, nch)

    rest = ((nch_total - nch) * KCHUNK).astype(f32)
    part = 32

    def count_ge(cand):
        def body(c, cnt):
            ind = jnp.where(keys_t[c] >= cand, 1.0, 0.0)
            return cnt + jnp.sum(ind.reshape(KCHUNK // part, part, BLOCK), axis=0)
        cnt = lax.fori_loop(0, nch, body, jnp.zeros((part, BLOCK), f32))
        tot = jnp.sum(cnt, axis=0, keepdims=True)
        return tot + jnp.where(cand <= NEG_KEY, rest, 0.0)

    kf = jnp.float32(topk)
    tau0 = jnp.where(count_ge(jnp.zeros((1, BLOCK), jnp.int32)) >= kf,
                     jnp.int32(0), jnp.int32(-2 ** 31))

    def bisect_body(t, tau):
        cand = tau + lax.shift_left(jnp.int32(1), 30 - t)
        return jnp.where(count_ge(cand) >= kf, cand, tau)

    tau = lax.fori_loop(0, 31, bisect_body, tau0)
    tau_rows = jnp.broadcast_to(tau, (BLOCK, BLOCK)).T
    tau_rows = jnp.concatenate([tau_rows, tau_rows], axis=1)

    mrun[...] = jnp.full(mrun.shape, NEG_INF, f32)
    nfar = jnp.maximum(nch - 2, 0)

    def pass1(c, near):
        start = pl.multiple_of(c * KCHUNK, KCHUNK)
        sel = keys[c] >= tau_rows
        if near:
            sel = jnp.logical_and(sel, start + kiota <= qpos)
            t0 = jnp.clip(2 * c - i + 2, 0, 2)
            t1 = jnp.clip(2 * c + 1 - i + 2, 0, 2)
        mb = jnp.where(sel, 0.0, NEG_INF)
        for h in range(B_HEADS):
            r0, r1 = h * BLOCK, (h + 1) * BLOCK
            s = s_buf[c, r0:r1, :] + mb
            if near:
                s = s + jnp.concatenate([bias_ref[t0, h], bias_ref[t1, h]], axis=1)
            s_buf[c, r0:r1, :] = s
            mrun[r0:r1, :] = jnp.maximum(mrun[r0:r1, :], jnp.maximum(s[:, :LANES], s[:, LANES:]))

    def pass1_far(c, carry):
        pass1(c, False)
        return carry

    def pass1_near(c, carry):
        pass1(c, True)
        return carry

    lax.fori_loop(0, nfar, pass1_far, 0)
    lax.fori_loop(nfar, nch, pass1_near, 0)

    for h in range(B_HEADS):
        r0, r1 = h * BLOCK, (h + 1) * BLOCK
        mrun[r0:r1, :] = jnp.broadcast_to(jnp.max(mrun[r0:r1, :], axis=-1, keepdims=True), (BLOCK, LANES))

    def pass2(c, first=False):
        start = c * KCHUNK if first else pl.multiple_of(c * KCHUNK, KCHUNK)
        for h in range(B_HEADS):
            r0, r1 = h * BLOCK, (h + 1) * BLOCK
            m = mrun[r0:r1, :]
            p = jnp.exp2(s_buf[c, r0:r1, :] - jnp.concatenate([m, m], axis=1))
            psum = p[:, :LANES] + p[:, LANES:]
            lrun[r0:r1, :] = psum if first else lrun[r0:r1, :] + psum
            p_buf[r0:r1, :] = p.astype(bf16)
        pv = _dot(p_buf[...], ckv_ref[0, pl.ds(start, KCHUNK), :])
        acc[...] = pv if first else acc[...] + pv

    pass2(0, first=True)
    _chunk_pairs(pass2, 1, nch)

    for pr in range(B_HEADS // 2):
        parts = []
        for par in (0, 1):
            r0 = (2 * pr + par) * BLOCK
            l = jnp.sum(lrun[r0:r0 + BLOCK, :], axis=-1, keepdims=True)
            parts.append((acc[r0:r0 + BLOCK, :] * (1.0 / l)).astype(bf16))
        o = _dot(jnp.concatenate(parts, axis=1), wuv_ref[pr])
        o_ref[0, :, pr * LANES:(pr + 1) * LANES] = o.astype(bf16)


def _dsa(qb, qi, wi, ki2, ckv, wuk_p, wuv_p, bias_t, l, topk):
    bsz, seq, _ = qb.shape
    nblk = seq // BLOCK
    nch_total = seq // KCHUNK
    rows_all = B_HEADS * BLOCK
    cur = lambda w: pl.BlockSpec((1, BLOCK, w), lambda b, i: (b, i, 0))
    whole = lambda w: pl.BlockSpec((1, seq, w), lambda b, i: (b, 0, 0))
    kern = functools.partial(_dsa_kernel, topk=topk, nch_total=nch_total)
    return pl.pallas_call(
        kern,
        grid=(bsz, nblk),
        in_specs=[
            cur(B_Q), cur(I_Q), cur(LANES), whole(2 * IDX_DIM), whole(KV_RANK),
            pl.BlockSpec((None,) + wuk_p.shape[1:], lambda b, i: (l, 0, 0, 0)),
            pl.BlockSpec((None,) + wuv_p.shape[1:], lambda b, i: (l, 0, 0, 0)),
            pl.BlockSpec(bias_t.shape, lambda b, i: (0, 0, 0, 0)),
        ],
        out_specs=cur(B_Q),
        out_shape=jax.ShapeDtypeStruct((bsz, seq, B_Q), bf16),
        scratch_shapes=[
            pltpu.VMEM((rows_all, LANES), bf16),
            pltpu.VMEM((rows_all, KV_RANK), bf16),
            pltpu.VMEM((IDX_HEADS, BLOCK, KCHUNK), f32),
            pltpu.VMEM((nch_total, BLOCK, KCHUNK), jnp.int32),
            pltpu.VMEM((nch_total, KCHUNK, BLOCK), jnp.int32),
            pltpu.VMEM((nch_total, rows_all, KCHUNK), f32),
            pltpu.VMEM((rows_all, KCHUNK), bf16),
            pltpu.VMEM((rows_all, KV_RANK), f32),
            pltpu.VMEM((rows_all, LANES), f32),
            pltpu.VMEM((rows_all, LANES), f32),
        ],
        compiler_params=_cparams(("arbitrary", "arbitrary")),
        name="dsa",
    )(qb, qi, wi, ki2, ckv, wuk_p, wuv_p, bias_t)


def _outproj_kernel(oa_ref, ob_ref, w_ref, x_ref, g_ref, o_ref):
    mix = _dot(oa_ref[...], w_ref[:A_Q, :]) + _dot(ob_ref[...], w_ref[A_Q:, :])
    o_ref[...] = x_ref[...] + g_ref[0] * mix


def _outproj(oa, ob, w_out, x2, g1, l, seq, tm):
    n, d = x2.shape
    rows = lambda w: pl.BlockSpec((tm, w), lambda i: (i, 0))
    return pl.pallas_call(
        _outproj_kernel,
        grid=(n // tm,),
        in_specs=[
            rows(A_Q), rows(B_Q),
            pl.BlockSpec((None, A_Q + B_Q, d), lambda i: (l, 0, 0), pipeline_mode=pl.Buffered(1)),
            rows(d),
            pl.BlockSpec((1, 1, d), lambda i: ((i * tm) // seq, 0, 0)),
        ],
        out_specs=rows(d),
        out_shape=jax.ShapeDtypeStruct((n, d), f32),
        compiler_params=_cparams(("arbitrary",)),
        name="outproj",
    )(oa, ob, w_out, x2, g1)


def _ffn_kernel(x_ref, xh_ref, sc_ref, sh_ref, gate_ref, g_ref, wu_ref, wg_ref, cw_ref, cb_ref, wd_ref, gf_ref,
                o_ref, hbuf, acc, *, seq, tm, final):
    i = pl.program_id(0)
    j = pl.program_id(1)

    @pl.when(j == 0)
    def _():
        hbuf[HALO:, :] = _rms_mod(x_ref[...], g_ref[...], sc_ref[0], sh_ref[0]).astype(bf16)
        hbuf[:HALO, :] = _rms_mod(xh_ref[...], g_ref[...], sc_ref[0], sh_ref[0]).astype(bf16)
        acc[...] = jnp.zeros(acc.shape, f32)

    u = _dot(hbuf[...], wu_ref[...])
    gt = _dot(hbuf[HALO:, :], wg_ref[...])
    tpos = lax.rem(i * tm, seq) + lax.broadcasted_iota(jnp.int32, (tm, 1), 0)
    u1 = jnp.where(tpos >= 1, u[HALO - 1:HALO - 1 + tm], 0.0)
    u2 = jnp.where(tpos >= 2, u[HALO - 2:HALO - 2 + tm], 0.0)
    cw = cw_ref[...]
    a = cw[2:3] * u[HALO:] + cw[1:2] * u1 + cw[0:1] * u2 + cb_ref[...]
    act = (a * _sigmoid(a) * gt).astype(bf16)
    acc[...] = acc[...] + _dot(act, wd_ref[...])

    @pl.when(j == pl.num_programs(1) - 1)
    def _():
        y = x_ref[...] + gate_ref[0] * acc[...]
        if final:
            y = y * lax.rsqrt(jnp.mean(y * y, axis=-1, keepdims=True) + EPS) * gf_ref[...]
        o_ref[...] = y


def _ffn(x2, sc, sh, gate, g, w_up, conv_w, conv_b, w_down, g_final, l, seq, tm, tf, final):
    n, d = x2.shape
    ff = w_down.shape[1]
    nf = ff // tf
    per_b = pl.BlockSpec((1, 1, d), lambda i, j: ((i * tm) // seq, 0, 0))
    halo_blocks = tm // HALO
    kern = functools.partial(_ffn_kernel, seq=seq, tm=tm, final=final)
    return pl.pallas_call(
        kern,
        grid=(n // tm, nf),
        in_specs=[
            pl.BlockSpec((tm, d), lambda i, j: (i, 0)),
            pl.BlockSpec((HALO, d), lambda i, j: (jnp.maximum(i * halo_blocks - 1, 0), 0)),
            per_b, per_b, per_b,
            pl.BlockSpec((None, 1, d), lambda i, j: (l, 0, 0)),
            pl.BlockSpec((None, d, tf), lambda i, j: (l, 0, j)),
            pl.BlockSpec((None, d, tf), lambda i, j: (l, 0, nf + j)),
            pl.BlockSpec((None, 3, tf), lambda i, j: (l, 0, j)),
            pl.BlockSpec((None, 1, tf), lambda i, j: (l, 0, j)),
            pl.BlockSpec((None, tf, d), lambda i, j: (l, j, 0)),
            pl.BlockSpec((1, d), lambda i, j: (0, 0)),
        ],
        out_specs=pl.BlockSpec((tm, d), lambda i, j: (i, 0)),
        out_shape=jax.ShapeDtypeStruct((n, d), f32),
        scratch_shapes=[pltpu.VMEM((tm + HALO, d), bf16), pltpu.VMEM((tm, d), f32)],
        compiler_params=_cparams(("arbitrary", "arbitrary")),
        name="ffn",
    )(x2, x2, sc, sh, gate, g, w_up, w_up, conv_w, conv_b, w_down, g_final)


def _rel_bucket(dist):
    n = jnp.maximum(dist, 0)
    max_exact = NUM_BUCKETS // 2
    nf = jnp.maximum(n, 1).astype(f32)
    large = max_exact + (jnp.log(nf / max_exact) / math.log(MAX_DISTANCE / max_exact)
                         * (NUM_BUCKETS - max_exact)).astype(jnp.int32)
    large = jnp.minimum(large, NUM_BUCKETS - 1)
    return jnp.where(n < max_exact, n, large)


def _pack_w_in(w_in):
    o_ka = A_Q
    o_va = o_ka + A_KV
    o_qb = o_va + A_KV
    o_ckv = o_qb + B_Q
    o_qi = o_ckv + KV_RANK
    o_ki = o_qi + I_Q
    o_wi = o_ki + IDX_DIM
    sl = lambda a, b: w_in[:, :, a:b]
    dup = lambda a: [sl(a, a + HEAD_DIM)] * 2
    ki = sl(o_ki, o_wi)
    wi = sl(o_wi, o_wi + IDX_HEADS)
    pad = jnp.zeros(w_in.shape[:2] + (LANES - IDX_HEADS,), w_in.dtype)
    cols = ([sl(0, A_Q), sl(o_qb, o_ckv), sl(o_qi, o_ki), sl(o_ckv, o_qi)]
            + dup(o_ka) + dup(o_ka + HEAD_DIM) + dup(o_va) + dup(o_va + HEAD_DIM)
            + [ki, ki, wi, pad])
    return jnp.concatenate(cols, axis=-1).astype(bf16)


def _swa_tables(rel_a, sinks):
    t = jnp.arange(BLOCK, dtype=jnp.int32)[:, None]
    j = jnp.arange(2 * BLOCK, dtype=jnp.int32)[None, :]
    dist = t + BLOCK - j
    valid = (dist >= 0) & (dist < BLOCK)
    tab = jnp.where(valid[None], jnp.transpose(rel_a[_rel_bucket(dist)], (2, 0, 1)), NEG_INF)
    pairs = A_HEADS // A_KV_HEADS // 2
    order = [g * 2 * pairs + 2 * pr + par for g in range(A_KV_HEADS) for par in (0, 1) for pr in range(pairs)]
    tab = tab[jnp.array(order)].reshape(A_KV_HEADS, 2 * pairs * BLOCK, 2 * BLOCK).astype(f32)
    sink = jnp.repeat(sinks[:, jnp.array(order)], BLOCK, axis=1).reshape(-1, A_KV_HEADS, 2 * pairs * BLOCK, 1)
    return tab, jnp.broadcast_to(sink, sink.shape[:-1] + (LANES,)).astype(f32)


def _dsa_tables(rel_b):
    t = jnp.arange(BLOCK, dtype=jnp.int32)[:, None]
    j = jnp.arange(BLOCK, dtype=jnp.int32)[None, :]
    far = rel_b[NUM_BUCKETS - 1][:, None, None]
    prev = jnp.transpose(rel_b[_rel_bucket(t + BLOCK - j)], (2, 0, 1)) - far
    diag = jnp.transpose(rel_b[_rel_bucket(t - j)], (2, 0, 1)) - far
    return (jnp.stack([jnp.zeros_like(prev), prev, diag]) * LOG2E).astype(f32)


def _pick(n, prefs):
    for p in prefs:
        if n % p == 0:
            return p
    return n


def kernel(x, c, w_mod, b_mod, norm_attn, w_in, attn_sinks, kv_norm, w_uk, w_uv, rel_bias, w_out,
           norm_ffn, w_up, conv_w, conv_b, w_down, norm_final):
    bsz, seq, d = x.shape
    depth = w_in.shape[0]
    ff = w_down.shape[1]
    n = bsz * seq
    assert seq % KCHUNK == 0 and d % LANES == 0 and ff % LANES == 0
    topk = min(TOPK_MAX, seq // 4)
    tm = _pick(seq, (512, 256, 128))
    tf = _pick(ff, (512, 256, 128))

    w_in_p = _pack_w_in(w_in)
    w_out_b = w_out.astype(bf16)
    w_up_b = w_up.astype(bf16)
    w_down_b = w_down.astype(bf16)
    wuk_p = jnp.swapaxes(w_uk, 2, 3).reshape(depth, B_HEADS // 2, 2 * HEAD_DIM, KV_RANK).astype(bf16)
    z = jnp.zeros_like(w_uv[:, 0::2])
    wuv_p = jnp.concatenate([jnp.concatenate([w_uv[:, 0::2], z], axis=-1),
                             jnp.concatenate([z, w_uv[:, 1::2]], axis=-1)], axis=2).astype(bf16)
    swa_bias, swa_sink = _swa_tables(rel_bias[:, :A_HEADS], attn_sinks)
    dsa_bias = _dsa_tables(rel_bias[:, A_HEADS:])
    norm_attn3 = norm_attn.reshape(depth, 1, d)
    norm_ffn3 = norm_ffn.reshape(depth, 1, d)
    kv_norm3 = kv_norm.reshape(depth, 1, KV_RANK)
    conv_b3 = conv_b.reshape(depth, 1, ff)

    mod = _modulation(c, w_mod, b_mod)
    x2 = x.reshape(n, d)
    for l in range(depth):
        sh1, sc1, g1, sh2, sc2, g2 = [mod[l, :, k * d:(k + 1) * d].reshape(bsz, 1, d) for k in range(6)]
        qa, qb, qi, ckv, ka2, va2, ki2, wi = _inproj(x2, sc1, sh1, norm_attn3, w_in_p, kv_norm3, l, seq, tm)
        r3 = lambda a: a.reshape(bsz, seq, a.shape[-1])
        out_a = _swa(r3(qa), r3(ka2), r3(va2), swa_bias, swa_sink[l])
        out_b = _dsa(r3(qb), r3(qi), r3(wi), r3(ki2), r3(ckv), wuk_p, wuv_p, dsa_bias, l, topk)
        x2 = _outproj(out_a.reshape(n, A_Q), out_b.reshape(n, B_Q), w_out_b, x2, g1, l, seq, tm)
        x2 = _ffn(x2, sc2, sh2, g2, norm_ffn3, w_up_b, conv_w, conv_b3, w_down_b, norm_final.reshape(1, d),
                  l, seq, tm, tf, final=(l == depth - 1))
    return x2.reshape(bsz, seq, d)
```

```python
import functools
import math

import jax
import jax.numpy as jnp
import numpy as np
from jax import lax
from jax.experimental import pallas as pl
from jax.experimental.pallas import tpu as pltpu

HEAD_DIM = 64
A_HEADS = 16
A_KV_HEADS = 2
BLOCK = 128
B_HEADS = 16
KV_RANK = 256
IDX_HEADS = 16
IDX_DIM = 64
TOPK_MAX = 256
NUM_BUCKETS = 32
MAX_DISTANCE = 128
EPS = 1e-6
NEG_INF = -1e30
IDX_SCALE = IDX_DIM ** -0.5 * IDX_HEADS ** -0.5
QK_SCALE = HEAD_DIM ** -0.5
LOG2E = math.log2(math.e)

A_Q = A_HEADS * HEAD_DIM
A_KV = A_KV_HEADS * HEAD_DIM
B_Q = B_HEADS * HEAD_DIM
I_Q = IDX_HEADS * IDX_DIM

LANES = 128
KCHUNK = 256
HALO = 16
VMEM_LIMIT = 56 * 1024 * 1024

C_QA = 0
C_QB = C_QA + A_Q
C_QI = C_QB + B_Q
C_CKV = C_QI + I_Q
C_KA = C_CKV + KV_RANK
C_VA = C_KA + 2 * A_KV
C_KI = C_VA + 2 * A_KV
C_WI = C_KI + 2 * IDX_DIM
IN_PACKED = C_WI + LANES

bf16 = jnp.bfloat16
f32 = jnp.float32


def _cparams(sem):
    return pltpu.CompilerParams(dimension_semantics=sem, vmem_limit_bytes=VMEM_LIMIT)


def _dot(a, b):
    return jnp.dot(a, b, preferred_element_type=f32)


def _dot_nt(a, b):
    return lax.dot_general(a, b, (((1,), (1,)), ((), ())), preferred_element_type=f32)


def _sigmoid(x):
    return 1.0 / (1.0 + jnp.exp(-x))


def _order_key(bits):
    return bits ^ ((bits >> 31) & jnp.int32(0x7FFFFFFF))


_NEG_BITS = int(np.float32(NEG_INF).view(np.int32))
NEG_KEY = _NEG_BITS ^ ((_NEG_BITS >> 31) & 0x7FFFFFFF)


def _rms_mod(x, g, sc, sh):
    ms = jnp.mean(x * x, axis=-1, keepdims=True)
    y = x * lax.rsqrt(ms + EPS) * g
    return y * (1.0 + sc) + sh


def _mod_kernel(c_ref, w_ref, b_ref, o_ref):
    c = c_ref[...]
    ca = (c * _sigmoid(c)).astype(bf16)
    o_ref[...] = _dot(ca, w_ref[...].astype(bf16)) + b_ref[...]


def _modulation(c, w_mod, b_mod):
    depth, d, n6 = w_mod.shape
    bsz = c.shape[0]
    tn = 1536 if n6 % 1536 == 0 else n6
    return pl.pallas_call(
        _mod_kernel,
        grid=(depth, n6 // tn),
        in_specs=[
            pl.BlockSpec((bsz, d), lambda l, j: (0, 0)),
            pl.BlockSpec((None, d, tn), lambda l, j: (l, 0, j)),
            pl.BlockSpec((None, 1, tn), lambda l, j: (l, 0, j)),
        ],
        out_specs=pl.BlockSpec((None, bsz, tn), lambda l, j: (l, 0, j)),
        out_shape=jax.ShapeDtypeStruct((depth, bsz, n6), f32),
        compiler_params=_cparams(("arbitrary", "arbitrary")),
        name="modulation",
    )(c, w_mod, b_mod.reshape(depth, 1, n6))


def _inproj_kernel(x_ref, sc_ref, sh_ref, g_ref, w_ref, kvg_ref,
                   qa_ref, qb_ref, qi_ref, ckv_ref, ka_ref, va_ref, ki_ref, wi_ref):
    h = _rms_mod(x_ref[...], g_ref[...], sc_ref[0], sh_ref[0]).astype(bf16)

    def proj(c0, c1):
        return _dot(h, w_ref[:, c0:c1])

    qa_ref[...] = proj(C_QA, C_QB).astype(bf16)
    qb_ref[...] = proj(C_QB, C_QI).astype(bf16)
    qi_ref[...] = proj(C_QI, C_CKV).astype(bf16)
    ckv = proj(C_CKV, C_KA)
    ckv = ckv * lax.rsqrt(jnp.mean(ckv * ckv, axis=-1, keepdims=True) + EPS) * kvg_ref[...]
    ckv_ref[...] = ckv.astype(bf16)
    ka_ref[...] = proj(C_KA, C_VA).astype(bf16)
    va_ref[...] = proj(C_VA, C_KI).astype(bf16)
    ki_ref[...] = proj(C_KI, C_WI).astype(bf16)
    wi_ref[...] = proj(C_WI, IN_PACKED)


def _inproj(x2, sc, sh, g, w_packed, kvg, l, seq, tm):
    n, d = x2.shape
    rows = lambda w: pl.BlockSpec((tm, w), lambda i: (i, 0))
    per_b = pl.BlockSpec((1, 1, d), lambda i: ((i * tm) // seq, 0, 0))
    widths = (A_Q, B_Q, I_Q, KV_RANK, 2 * A_KV, 2 * A_KV, 2 * IDX_DIM, LANES)
    dtypes = (bf16,) * 7 + (f32,)
    return pl.pallas_call(
        _inproj_kernel,
        grid=(n // tm,),
        in_specs=[
            rows(d), per_b, per_b,
            pl.BlockSpec((None, 1, d), lambda i: (l, 0, 0)),
            pl.BlockSpec((None, d, IN_PACKED), lambda i: (l, 0, 0), pipeline_mode=pl.Buffered(1)),
            pl.BlockSpec((None, 1, KV_RANK), lambda i: (l, 0, 0)),
        ],
        out_specs=[rows(w) for w in widths],
        out_shape=[jax.ShapeDtypeStruct((n, w), dt) for w, dt in zip(widths, dtypes)],
        compiler_params=_cparams(("arbitrary",)),
        name="inproj",
    )(x2, sc, sh, g, w_packed, kvg)


def _swa_kernel(q_ref, kp_ref, kc_ref, vp_ref, vc_ref, bias_ref, sink_ref, o_ref):
    i = pl.program_id(1)
    q = q_ref[0]
    kk = jnp.concatenate([kp_ref[0], kc_ref[0]], axis=0)
    vv = jnp.concatenate([vp_ref[0], vc_ref[0]], axis=0)
    lo_q = lax.broadcasted_iota(jnp.int32, (BLOCK, LANES), 1) < HEAD_DIM
    lo_v = lax.broadcasted_iota(jnp.int32, (2 * BLOCK, LANES), 1) < HEAD_DIM
    col = lax.broadcasted_iota(jnp.int32, (1, 2 * BLOCK), 1)
    no_prev = jnp.logical_and(i == 0, col < BLOCK)
    zero = jnp.zeros((), bf16)
    pairs = A_HEADS // A_KV_HEADS // 2
    half = pairs * BLOCK
    for g in range(A_KV_HEADS):
        rows = []
        for par in (0, 1):
            for pr in range(pairs):
                qp = q[:, (g * pairs + pr) * LANES:(g * pairs + pr + 1) * LANES]
                rows.append(jnp.where(lo_q, qp, zero) if par == 0 else jnp.where(lo_q, zero, qp))
        qs = jnp.concatenate(rows, axis=0)
        s = _dot_nt(qs, kk[:, g * LANES:(g + 1) * LANES]) * QK_SCALE + bias_ref[g]
        s = jnp.where(no_prev, NEG_INF, s)
        sink = sink_ref[g]
        nrow = sink.shape[0]
        m = jnp.maximum(jnp.broadcast_to(jnp.max(s, axis=-1, keepdims=True), (nrow, LANES)), sink)
        e = jnp.exp(s - jnp.concatenate([m, m], axis=1))
        r = 1.0 / (jnp.broadcast_to(jnp.sum(e, axis=-1, keepdims=True), (nrow, LANES)) + jnp.exp(sink - m))
        eb = e.astype(bf16)
        v2 = vv[:, g * LANES:(g + 1) * LANES]
        o = (_dot(eb[:half], jnp.where(lo_v, v2, zero)) * r[:half]
             + _dot(eb[half:], jnp.where(lo_v, zero, v2)) * r[half:])
        for pr in range(pairs):
            c0 = (g * pairs + pr) * LANES
            o_ref[0, :, c0:c0 + LANES] = o[pr * BLOCK:(pr + 1) * BLOCK].astype(bf16)


def _swa(qa, ka2, va2, bias, sink):
    bsz, seq, _ = qa.shape
    nblk = seq // BLOCK
    cur = lambda w: pl.BlockSpec((1, BLOCK, w), lambda b, i: (b, i, 0))
    prev = lambda w: pl.BlockSpec((1, BLOCK, w), lambda b, i: (b, jnp.maximum(i - 1, 0), 0))
    return pl.pallas_call(
        _swa_kernel,
        grid=(bsz, nblk),
        in_specs=[
            cur(A_Q), prev(2 * A_KV), cur(2 * A_KV), prev(2 * A_KV), cur(2 * A_KV),
            pl.BlockSpec(bias.shape, lambda b, i: (0, 0, 0)),
            pl.BlockSpec(sink.shape, lambda b, i: (0, 0, 0)),
        ],
        out_specs=cur(A_Q),
        out_shape=jax.ShapeDtypeStruct((bsz, seq, A_Q), bf16),
        compiler_params=_cparams(("arbitrary", "arbitrary")),
        name="swa",
    )(qa, ka2, ka2, va2, va2, bias, sink)


def _chunk_pairs(body, lo, hi):
    n = hi - lo

    def two(k, carry):
        body(lo + 2 * k)
        body(lo + 2 * k + 1)
        return carry

    lax.fori_loop(0, lax.shift_right_logical(n, 1), two, 0)

    @pl.when((n & 1) == 1)
    def _():
        body(hi - 1)


def _dsa_kernel(qb_ref, qi_ref, wi_ref, ki_ref, ckv_ref, wuk_ref, wuv_ref, bias_ref, o_ref,
                qstk, qlat, wb, keys, keys_t, jlim, s_buf, p_buf, acc, mrun, lrun, *, topk, nch_total):
    i = pl.program_id(1)
    nch = i // 2 + 1
    rows_all = B_HEADS * BLOCK
    lo = lax.broadcasted_iota(jnp.int32, (BLOCK, LANES), 1) < HEAD_DIM
    zero = jnp.zeros((), bf16)

    qi = qi_ref[0]
    qb = qb_ref[0]
    for pr in range(B_HEADS // 2):
        qp = qi[:, pr * LANES:(pr + 1) * LANES]
        qstk[(2 * pr) * BLOCK:(2 * pr + 1) * BLOCK, :] = jnp.where(lo, qp, zero)
        qstk[(2 * pr + 1) * BLOCK:(2 * pr + 2) * BLOCK, :] = jnp.where(lo, zero, qp)
        qp = qb[:, pr * LANES:(pr + 1) * LANES]
        w = wuk_ref[pr]
        qlat[(2 * pr) * BLOCK:(2 * pr + 1) * BLOCK, :] = (
            _dot(jnp.where(lo, qp, zero), w) * (QK_SCALE * LOG2E)).astype(bf16)
        qlat[(2 * pr + 1) * BLOCK:(2 * pr + 2) * BLOCK, :] = (
            _dot(jnp.where(lo, zero, qp), w) * (QK_SCALE * LOG2E)).astype(bf16)
    wi = wi_ref[0]
    for h in range(IDX_HEADS):
        wb[h] = jnp.broadcast_to(wi[:, h:h + 1] * IDX_SCALE, (BLOCK, KCHUNK))

    qpos = i * BLOCK + lax.broadcasted_iota(jnp.int32, (BLOCK, KCHUNK), 0)
    kiota = lax.broadcasted_iota(jnp.int32, (BLOCK, KCHUNK), 1)

    def score_body(c):
        start = pl.multiple_of(c * KCHUNK, KCHUNK)
        d = _dot_nt(qstk[...], ki_ref[0, pl.ds(start, KCHUNK), :])
        s_buf[c] = _dot_nt(qlat[...], ckv_ref[0, pl.ds(start, KCHUNK), :])
        sc = jnp.zeros((BLOCK, KCHUNK), f32)
        for h in range(IDX_HEADS):
            sc = sc + wb[h] * jnp.maximum(d[h * BLOCK:(h + 1) * BLOCK], 0.0)
        sc = jnp.where(start + kiota <= qpos, sc, NEG_INF)
        keys[c] = _order_key(pltpu.bitcast(sc, jnp.int32))
        keys_t[c] = _order_key(pltpu.bitcast(sc.T, jnp.int32))

    _chunk_pairs(score_body, 0, nch)

    rest = ((nch_total - nch) * KCHUNK).astype(f32)
    part = 32

    def count_where(pred):
        def body(c, cnt):
            ind = jnp.where(pred(c, keys_t[c]), 1.0, 0.0)
            return cnt + jnp.sum(ind.reshape(KCHUNK // part, part, BLOCK), axis=0)
        cnt = lax.fori_loop(0, nch, body, jnp.zeros((part, BLOCK), f32))
        return jnp.sum(cnt, axis=0, keepdims=True)

    def count_ge(cand):
        return count_where(lambda c, k: k >= cand) + jnp.where(cand <= NEG_KEY, rest, 0.0)

    kf = jnp.float32(topk)
    tau0 = jnp.where(count_ge(jnp.zeros((1, BLOCK), jnp.int32)) >= kf,
                     jnp.int32(0), jnp.int32(-2 ** 31))

    def bisect_body(t, tau):
        cand = tau + lax.shift_left(jnp.int32(1), 30 - t)
        return jnp.where(count_ge(cand) >= kf, cand, tau)

    tau = lax.fori_loop(0, 31, bisect_body, tau0)
    seq_len = nch_total * KCHUNK
    jlim[...] = jnp.full(jlim.shape, seq_len, jnp.int32)
    tie = jnp.logical_and(count_ge(tau) > kf, tau > NEG_KEY)

    @pl.when(jnp.max(jnp.where(tie, 1.0, 0.0)) > 0.0)
    def _():
        need = kf - count_where(lambda c, k: k > tau)
        kpos_t = lax.broadcasted_iota(jnp.int32, (KCHUNK, BLOCK), 0)

        def eq_below(v):
            return count_where(lambda c, k: jnp.logical_and(k == tau, c * KCHUNK + kpos_t < v))

        def pos_body(t, v):
            cand = v + lax.shift_left(jnp.int32(1), (seq_len - 1).bit_length() - 1 - t)
            return jnp.where(eq_below(cand) < need, cand, v)

        v = lax.fori_loop(0, (seq_len - 1).bit_length(), pos_body, jnp.zeros((1, BLOCK), jnp.int32))
        jlim[0:1, :] = jnp.where(tie, v + 1, seq_len)

    def to_rows(x):
        r = jnp.broadcast_to(x, (BLOCK, BLOCK)).T
        return jnp.concatenate([r, r], axis=1)

    tau_rows = to_rows(tau)
    jlim_rows = to_rows(jlim[0:1, :])

    mrun[...] = jnp.full(mrun.shape, NEG_INF, f32)
    nfar = jnp.maximum(nch - 2, 0)

    def pass1(c, near):
        start = pl.multiple_of(c * KCHUNK, KCHUNK)
        kc = keys[c]
        sel = jnp.logical_or(kc > tau_rows, jnp.logical_and(kc == tau_rows, start + kiota < jlim_rows))
        if near:
            sel = jnp.logical_and(sel, start + kiota <= qpos)
            t0 = jnp.clip(2 * c - i + 2, 0, 2)
            t1 = jnp.clip(2 * c + 1 - i + 2, 0, 2)
        mb = jnp.where(sel, 0.0, NEG_INF)
        for h in range(B_HEADS):
            r0, r1 = h * BLOCK, (h + 1) * BLOCK
            s = s_buf[c, r0:r1, :] + mb
            if near:
                s = s + jnp.concatenate([bias_ref[t0, h], bias_ref[t1, h]], axis=1)
            s_buf[c, r0:r1, :] = s
            mrun[r0:r1, :] = jnp.maximum(mrun[r0:r1, :], jnp.maximum(s[:, :LANES], s[:, LANES:]))

    def pass1_far(c, carry):
        pass1(c, False)
        return carry

    def pass1_near(c, carry):
        pass1(c, True)
        return carry

    lax.fori_loop(0, nfar, pass1_far, 0)
    lax.fori_loop(nfar, nch, pass1_near, 0)

    for h in range(B_HEADS):
        r0, r1 = h * BLOCK, (h + 1) * BLOCK
        mrun[r0:r1, :] = jnp.broadcast_to(jnp.max(mrun[r0:r1, :], axis=-1, keepdims=True), (BLOCK, LANES))

    def pass2(c, first=False):
        start = c * KCHUNK if first else pl.multiple_of(c * KCHUNK, KCHUNK)
        for h in range(B_HEADS):
            r0, r1 = h * BLOCK, (h + 1) * BLOCK
            m = mrun[r0:r1, :]
            p = jnp.exp2(s_buf[c, r0:r1, :] - jnp.concatenate([m, m], axis=1))
            psum = p[:, :LANES] + p[:, LANES:]
            lrun[r0:r1, :] = psum if first else lrun[r0:r1, :] + psum
            p_buf[r0:r1, :] = p.astype(bf16)
        pv = _dot(p_buf[...], ckv_ref[0, pl.ds(start, KCHUNK), :])
        acc[...] = pv if first else acc[...] + pv

    pass2(0, first=True)
    _chunk_pairs(pass2, 1, nch)

    for pr in range(B_HEADS // 2):
        parts = []
        for par in (0, 1):
            r0 = (2 * pr + par) * BLOCK
            l = jnp.sum(lrun[r0:r0 + BLOCK, :], axis=-1, keepdims=True)
            parts.append((acc[r0:r0 + BLOCK, :] * (1.0 / l)).astype(bf16))
        o = _dot(jnp.concatenate(parts, axis=1), wuv_ref[pr])
        o_ref[0, :, pr * LANES:(pr + 1) * LANES] = o.astype(bf16)


def _dsa(qb, qi, wi, ki2, ckv, wuk_p, wuv_p, bias_t, l, topk):
    bsz, seq, _ = qb.shape
    nblk = seq // BLOCK
    nch_total = seq // KCHUNK
    rows_all = B_HEADS * BLOCK
    cur = lambda w: pl.BlockSpec((1, BLOCK, w), lambda b, i: (b, i, 0))
    whole = lambda w: pl.BlockSpec((1, seq, w), lambda b, i: (b, 0, 0))
    kern = functools.partial(_dsa_kernel, topk=topk, nch_total=nch_total)
    return pl.pallas_call(
        kern,
        grid=(bsz, nblk),
        in_specs=[
            cur(B_Q), cur(I_Q), cur(LANES), whole(2 * IDX_DIM), whole(KV_RANK),
            pl.BlockSpec((None,) + wuk_p.shape[1:], lambda b, i: (l, 0, 0, 0)),
            pl.BlockSpec((None,) + wuv_p.shape[1:], lambda b, i: (l, 0, 0, 0)),
            pl.BlockSpec(bias_t.shape, lambda b, i: (0, 0, 0, 0)),
        ],
        out_specs=cur(B_Q),
        out_shape=jax.ShapeDtypeStruct((bsz, seq, B_Q), bf16),
        scratch_shapes=[
            pltpu.VMEM((rows_all, LANES), bf16),
            pltpu.VMEM((rows_all, KV_RANK), bf16),
            pltpu.VMEM((IDX_HEADS, BLOCK, KCHUNK), f32),
            pltpu.VMEM((nch_total, BLOCK, KCHUNK), jnp.int32),
            pltpu.VMEM((nch_total, KCHUNK, BLOCK), jnp.int32),
            pltpu.VMEM((8, BLOCK), jnp.int32),
            pltpu.VMEM((nch_total, rows_all, KCHUNK), f32),
            pltpu.VMEM((rows_all, KCHUNK), bf16),
            pltpu.VMEM((rows_all, KV_RANK), f32),
            pltpu.VMEM((rows_all, LANES), f32),
            pltpu.VMEM((rows_all, LANES), f32),
        ],
        compiler_params=_cparams(("arbitrary", "arbitrary")),
        name="dsa",
    )(qb, qi, wi, ki2, ckv, wuk_p, wuv_p, bias_t)


def _outproj_kernel(oa_ref, ob_ref, w_ref, x_ref, g_ref, o_ref):
    mix = _dot(oa_ref[...], w_ref[:A_Q, :]) + _dot(ob_ref[...], w_ref[A_Q:, :])
    o_ref[...] = x_ref[...] + g_ref[0] * mix


def _outproj(oa, ob, w_out, x2, g1, l, seq, tm):
    n, d = x2.shape
    rows = lambda w: pl.BlockSpec((tm, w), lambda i: (i, 0))
    return pl.pallas_call(
        _outproj_kernel,
        grid=(n // tm,),
        in_specs=[
            rows(A_Q), rows(B_Q),
            pl.BlockSpec((None, A_Q + B_Q, d), lambda i: (l, 0, 0), pipeline_mode=pl.Buffered(1)),
            rows(d),
            pl.BlockSpec((1, 1, d), lambda i: ((i * tm) // seq, 0, 0)),
        ],
        out_specs=rows(d),
        out_shape=jax.ShapeDtypeStruct((n, d), f32),
        compiler_params=_cparams(("arbitrary",)),
        name="outproj",
    )(oa, ob, w_out, x2, g1)


def _ffn_kernel(x_ref, xh_ref, sc_ref, sh_ref, gate_ref, g_ref, wu_ref, wg_ref, cw_ref, cb_ref, wd_ref, gf_ref,
                o_ref, hbuf, acc, *, seq, tm, final):
    i = pl.program_id(0)
    j = pl.program_id(1)

    @pl.when(j == 0)
    def _():
        hbuf[HALO:, :] = _rms_mod(x_ref[...], g_ref[...], sc_ref[0], sh_ref[0]).astype(bf16)
        hbuf[:HALO, :] = _rms_mod(xh_ref[...], g_ref[...], sc_ref[0], sh_ref[0]).astype(bf16)
        acc[...] = jnp.zeros(acc.shape, f32)

    u = _dot(hbuf[...], wu_ref[...])
    gt = _dot(hbuf[HALO:, :], wg_ref[...])
    tpos = lax.rem(i * tm, seq) + lax.broadcasted_iota(jnp.int32, (tm, 1), 0)
    u1 = jnp.where(tpos >= 1, u[HALO - 1:HALO - 1 + tm], 0.0)
    u2 = jnp.where(tpos >= 2, u[HALO - 2:HALO - 2 + tm], 0.0)
    cw = cw_ref[...]
    a = cw[2:3] * u[HALO:] + cw[1:2] * u1 + cw[0:1] * u2 + cb_ref[...]
    act = (a * _sigmoid(a) * gt).astype(bf16)
    acc[...] = acc[...] + _dot(act, wd_ref[...])

    @pl.when(j == pl.num_programs(1) - 1)
    def _():
        y = x_ref[...] + gate_ref[0] * acc[...]
        if final:
            y = y * lax.rsqrt(jnp.mean(y * y, axis=-1, keepdims=True) + EPS) * gf_ref[...]
        o_ref[...] = y


def _ffn(x2, sc, sh, gate, g, w_up, conv_w, conv_b, w_down, g_final, l, seq, tm, tf, final):
    n, d = x2.shape
    ff = w_down.shape[1]
    nf = ff // tf
    per_b = pl.BlockSpec((1, 1, d), lambda i, j: ((i * tm) // seq, 0, 0))
    halo_blocks = tm // HALO
    kern = functools.partial(_ffn_kernel, seq=seq, tm=tm, final=final)
    return pl.pallas_call(
        kern,
        grid=(n // tm, nf),
        in_specs=[
            pl.BlockSpec((tm, d), lambda i, j: (i, 0)),
            pl.BlockSpec((HALO, d), lambda i, j: (jnp.maximum(i * halo_blocks - 1, 0), 0)),
            per_b, per_b, per_b,
            pl.BlockSpec((None, 1, d), lambda i, j: (l, 0, 0)),
            pl.BlockSpec((None, d, tf), lambda i, j: (l, 0, j)),
            pl.BlockSpec((None, d, tf), lambda i, j: (l, 0, nf + j)),
            pl.BlockSpec((None, 3, tf), lambda i, j: (l, 0, j)),
            pl.BlockSpec((None, 1, tf), lambda i, j: (l, 0, j)),
            pl.BlockSpec((None, tf, d), lambda i, j: (l, j, 0)),
            pl.BlockSpec((1, d), lambda i, j: (0, 0)),
        ],
        out_specs=pl.BlockSpec((tm, d), lambda i, j: (i, 0)),
        out_shape=jax.ShapeDtypeStruct((n, d), f32),
        scratch_shapes=[pltpu.VMEM((tm + HALO, d), bf16), pltpu.VMEM((tm, d), f32)],
        compiler_params=_cparams(("arbitrary", "arbitrary")),
        name="ffn",
    )(x2, x2, sc, sh, gate, g, w_up, w_up, conv_w, conv_b, w_down, g_final)


def _rel_bucket(dist):
    n = jnp.maximum(dist, 0)
    max_exact = NUM_BUCKETS // 2
    nf = jnp.maximum(n, 1).astype(f32)
    large = max_exact + (jnp.log(nf / max_exact) / math.log(MAX_DISTANCE / max_exact)
                         * (NUM_BUCKETS - max_exact)).astype(jnp.int32)
    large = jnp.minimum(large, NUM_BUCKETS - 1)
    return jnp.where(n < max_exact, n, large)


def _pack_w_in(w_in):
    o_ka = A_Q
    o_va = o_ka + A_KV
    o_qb = o_va + A_KV
    o_ckv = o_qb + B_Q
    o_qi = o_ckv + KV_RANK
    o_ki = o_qi + I_Q
    o_wi = o_ki + IDX_DIM
    sl = lambda a, b: w_in[:, :, a:b]
    dup = lambda a: [sl(a, a + HEAD_DIM)] * 2
    ki = sl(o_ki, o_wi)
    wi = sl(o_wi, o_wi + IDX_HEADS)
    pad = jnp.zeros(w_in.shape[:2] + (LANES - IDX_HEADS,), w_in.dtype)
    cols = ([sl(0, A_Q), sl(o_qb, o_ckv), sl(o_qi, o_ki), sl(o_ckv, o_qi)]
            + dup(o_ka) + dup(o_ka + HEAD_DIM) + dup(o_va) + dup(o_va + HEAD_DIM)
            + [ki, ki, wi, pad])
    return jnp.concatenate(cols, axis=-1).astype(bf16)


def _swa_tables(rel_a, sinks):
    t = jnp.arange(BLOCK, dtype=jnp.int32)[:, None]
    j = jnp.arange(2 * BLOCK, dtype=jnp.int32)[None, :]
    dist = t + BLOCK - j
    valid = (dist >= 0) & (dist < BLOCK)
    tab = jnp.where(valid[None], jnp.transpose(rel_a[_rel_bucket(dist)], (2, 0, 1)), NEG_INF)
    pairs = A_HEADS // A_KV_HEADS // 2
    order = [g * 2 * pairs + 2 * pr + par for g in range(A_KV_HEADS) for par in (0, 1) for pr in range(pairs)]
    tab = tab[jnp.array(order)].reshape(A_KV_HEADS, 2 * pairs * BLOCK, 2 * BLOCK).astype(f32)
    sink = jnp.repeat(sinks[:, jnp.array(order)], BLOCK, axis=1).reshape(-1, A_KV_HEADS, 2 * pairs * BLOCK, 1)
    return tab, jnp.broadcast_to(sink, sink.shape[:-1] + (LANES,)).astype(f32)


def _dsa_tables(rel_b):
    t = jnp.arange(BLOCK, dtype=jnp.int32)[:, None]
    j = jnp.arange(BLOCK, dtype=jnp.int32)[None, :]
    far = rel_b[NUM_BUCKETS - 1][:, None, None]
    prev = jnp.transpose(rel_b[_rel_bucket(t + BLOCK - j)], (2, 0, 1)) - far
    diag = jnp.transpose(rel_b[_rel_bucket(t - j)], (2, 0, 1)) - far
    return (jnp.stack([jnp.zeros_like(prev), prev, diag]) * LOG2E).astype(f32)


def _pick(n, prefs):
    for p in prefs:
        if n % p == 0:
            return p
    return n


def kernel(x, c, w_mod, b_mod, norm_attn, w_in, attn_sinks, kv_norm, w_uk, w_uv, rel_bias, w_out,
           norm_ffn, w_up, conv_w, conv_b, w_down, norm_final):
    bsz, seq, d = x.shape
    depth = w_in.shape[0]
    ff = w_down.shape[1]
    n = bsz * seq
    assert seq % KCHUNK == 0 and d % LANES == 0 and ff % LANES == 0
    topk = min(TOPK_MAX, seq // 4)
    tm = _pick(seq, (512, 256, 128))
    tf = _pick(ff, (512, 256, 128))

    w_in_p = _pack_w_in(w_in)
    w_out_b = w_out.astype(bf16)
    w_up_b = w_up.astype(bf16)
    w_down_b = w_down.astype(bf16)
    wuk_p = jnp.swapaxes(w_uk, 2, 3).reshape(depth, B_HEADS // 2, 2 * HEAD_DIM, KV_RANK).astype(bf16)
    z = jnp.zeros_like(w_uv[:, 0::2])
    wuv_p = jnp.concatenate([jnp.concatenate([w_uv[:, 0::2], z], axis=-1),
                             jnp.concatenate([z, w_uv[:, 1::2]], axis=-1)], axis=2).astype(bf16)
    swa_bias, swa_sink = _swa_tables(rel_bias[:, :A_HEADS], attn_sinks)
    dsa_bias = _dsa_tables(rel_bias[:, A_HEADS:])
    norm_attn3 = norm_attn.reshape(depth, 1, d)
    norm_ffn3 = norm_ffn.reshape(depth, 1, d)
    kv_norm3 = kv_norm.reshape(depth, 1, KV_RANK)
    conv_b3 = conv_b.reshape(depth, 1, ff)

    mod = _modulation(c, w_mod, b_mod)
    x2 = x.reshape(n, d)
    for l in range(depth):
        sh1, sc1, g1, sh2, sc2, g2 = [mod[l, :, k * d:(k + 1) * d].reshape(bsz, 1, d) for k in range(6)]
        qa, qb, qi, ckv, ka2, va2, ki2, wi = _inproj(x2, sc1, sh1, norm_attn3, w_in_p, kv_norm3, l, seq, tm)
        r3 = lambda a: a.reshape(bsz, seq, a.shape[-1])
        out_a = _swa(r3(qa), r3(ka2), r3(va2), swa_bias, swa_sink[l])
        out_b = _dsa(r3(qb), r3(qi), r3(wi), r3(ki2), r3(ckv), wuk_p, wuv_p, dsa_bias, l, topk)
        x2 = _outproj(out_a.reshape(n, A_Q), out_b.reshape(n, B_Q), w_out_b, x2, g1, l, seq, tm)
        x2 = _ffn(x2, sc2, sh2, g2, norm_ffn3, w_up_b, conv_w, conv_b3, w_down_b, norm_final.reshape(1, d),
                  l, seq, tm, tf, final=(l == depth - 1))
    return x2.reshape(bsz, seq, d)
```

```python
import functools
import math

import jax
import jax.numpy as jnp
import numpy as np
from jax import lax
from jax.experimental import pallas as pl
from jax.experimental.pallas import tpu as pltpu

HEAD_DIM = 64
A_HEADS = 16
A_KV_HEADS = 2
BLOCK = 128
B_HEADS = 16
KV_RANK = 256
IDX_HEADS = 16
IDX_DIM = 64
TOPK_MAX = 256
NUM_BUCKETS = 32
MAX_DISTANCE = 128
EPS = 1e-6
NEG_INF = -1e30
IDX_SCALE = IDX_DIM ** -0.5 * IDX_HEADS ** -0.5
QK_SCALE = HEAD_DIM ** -0.5
LOG2E = math.log2(math.e)

A_Q = A_HEADS * HEAD_DIM
A_KV = A_KV_HEADS * HEAD_DIM
B_Q = B_HEADS * HEAD_DIM
I_Q = IDX_HEADS * IDX_DIM

LANES = 128
KCHUNK = 256
HALO = 16
VMEM_LIMIT = 56 * 1024 * 1024

C_QA = 0
C_QB = C_QA + A_Q
C_QI = C_QB + B_Q
C_CKV = C_QI + I_Q
C_KA = C_CKV + KV_RANK
C_VA = C_KA + 2 * A_KV
C_KI = C_VA + 2 * A_KV
C_WI = C_KI + 2 * IDX_DIM
IN_PACKED = C_WI + LANES

bf16 = jnp.bfloat16
f32 = jnp.float32


def _cparams(sem):
    return pltpu.CompilerParams(dimension_semantics=sem, vmem_limit_bytes=VMEM_LIMIT)


def _dot(a, b):
    return jnp.dot(a, b, preferred_element_type=f32)


def _dot_nt(a, b):
    return lax.dot_general(a, b, (((1,), (1,)), ((), ())), preferred_element_type=f32)


def _sigmoid(x):
    return 1.0 / (1.0 + jnp.exp(-x))


def _order_key(bits):
    return bits ^ ((bits >> 31) & jnp.int32(0x7FFFFFFF))


_NEG_BITS = int(np.float32(NEG_INF).view(np.int32))
NEG_KEY = _NEG_BITS ^ ((_NEG_BITS >> 31) & 0x7FFFFFFF)


def _rms_mod(x, g, sc, sh):
    ms = jnp.mean(x * x, axis=-1, keepdims=True)
    y = x * lax.rsqrt(ms + EPS) * g
    return y * (1.0 + sc) + sh


def _mod_kernel(c_ref, w_ref, b_ref, o_ref):
    c = c_ref[...]
    ca = (c * _sigmoid(c)).astype(bf16)
    o_ref[...] = _dot(ca, w_ref[...].astype(bf16)) + b_ref[...]


def _modulation(c, w_mod, b_mod):
    depth, d, n6 = w_mod.shape
    bsz = c.shape[0]
    tn = 1536 if n6 % 1536 == 0 else n6
    return pl.pallas_call(
        _mod_kernel,
        grid=(depth, n6 // tn),
        in_specs=[
            pl.BlockSpec((bsz, d), lambda l, j: (0, 0)),
            pl.BlockSpec((None, d, tn), lambda l, j: (l, 0, j)),
            pl.BlockSpec((None, 1, tn), lambda l, j: (l, 0, j)),
        ],
        out_specs=pl.BlockSpec((None, bsz, tn), lambda l, j: (l, 0, j)),
        out_shape=jax.ShapeDtypeStruct((depth, bsz, n6), f32),
        compiler_params=_cparams(("arbitrary", "arbitrary")),
        name="modulation",
    )(c, w_mod, b_mod.reshape(depth, 1, n6))


def _inproj_kernel(x_ref, sc_ref, sh_ref, g_ref, w_ref, kvg_ref,
                   qa_ref, qb_ref, qi_ref, ckv_ref, ka_ref, va_ref, ki_ref, wi_ref):
    h = _rms_mod(x_ref[...], g_ref[...], sc_ref[0], sh_ref[0]).astype(bf16)

    def proj(c0, c1):
        return _dot(h, w_ref[:, c0:c1])

    qa_ref[...] = proj(C_QA, C_QB).astype(bf16)
    qb_ref[...] = proj(C_QB, C_QI).astype(bf16)
    qi_ref[...] = proj(C_QI, C_CKV).astype(bf16)
    ckv = proj(C_CKV, C_KA)
    ckv = ckv * lax.rsqrt(jnp.mean(ckv * ckv, axis=-1, keepdims=True) + EPS) * kvg_ref[...]
    ckv_ref[...] = ckv.astype(bf16)
    ka_ref[...] = proj(C_KA, C_VA).astype(bf16)
    va_ref[...] = proj(C_VA, C_KI).astype(bf16)
    ki_ref[...] = proj(C_KI, C_WI).astype(bf16)
    wi_ref[...] = proj(C_WI, IN_PACKED)


def _inproj(x2, sc, sh, g, w_packed, kvg, l, seq, tm):
    n, d = x2.shape
    rows = lambda w: pl.BlockSpec((tm, w), lambda i: (i, 0))
    per_b = pl.BlockSpec((1, 1, d), lambda i: ((i * tm) // seq, 0, 0))
    widths = (A_Q, B_Q, I_Q, KV_RANK, 2 * A_KV, 2 * A_KV, 2 * IDX_DIM, LANES)
    dtypes = (bf16,) * 7 + (f32,)
    return pl.pallas_call(
        _inproj_kernel,
        grid=(n // tm,),
        in_specs=[
            rows(d), per_b, per_b,
            pl.BlockSpec((None, 1, d), lambda i: (l, 0, 0)),
            pl.BlockSpec((None, d, IN_PACKED), lambda i: (l, 0, 0), pipeline_mode=pl.Buffered(1)),
            pl.BlockSpec((None, 1, KV_RANK), lambda i: (l, 0, 0)),
        ],
        out_specs=[rows(w) for w in widths],
        out_shape=[jax.ShapeDtypeStruct((n, w), dt) for w, dt in zip(widths, dtypes)],
        compiler_params=_cparams(("arbitrary",)),
        name="inproj",
    )(x2, sc, sh, g, w_packed, kvg)


def _swa_kernel(q_ref, kp_ref, kc_ref, vp_ref, vc_ref, bias_ref, sink_ref, o_ref):
    i = pl.program_id(1)
    q = q_ref[0]
    kk = jnp.concatenate([kp_ref[0], kc_ref[0]], axis=0)
    vv = jnp.concatenate([vp_ref[0], vc_ref[0]], axis=0)
    lo_q = lax.broadcasted_iota(jnp.int32, (BLOCK, LANES), 1) < HEAD_DIM
    lo_v = lax.broadcasted_iota(jnp.int32, (2 * BLOCK, LANES), 1) < HEAD_DIM
    col = lax.broadcasted_iota(jnp.int32, (1, 2 * BLOCK), 1)
    no_prev = jnp.logical_and(i == 0, col < BLOCK)
    zero = jnp.zeros((), bf16)
    pairs = A_HEADS // A_KV_HEADS // 2
    half = pairs * BLOCK
    for g in range(A_KV_HEADS):
        rows = []
        for par in (0, 1):
            for pr in range(pairs):
                qp = q[:, (g * pairs + pr) * LANES:(g * pairs + pr + 1) * LANES]
                rows.append(jnp.where(lo_q, qp, zero) if par == 0 else jnp.where(lo_q, zero, qp))
        qs = jnp.concatenate(rows, axis=0)
        s = _dot_nt(qs, kk[:, g * LANES:(g + 1) * LANES]) * QK_SCALE + bias_ref[g]
        s = jnp.where(no_prev, NEG_INF, s)
        sink = sink_ref[g]
        nrow = sink.shape[0]
        m = jnp.maximum(jnp.broadcast_to(jnp.max(s, axis=-1, keepdims=True), (nrow, LANES)), sink)
        e = jnp.exp(s - jnp.concatenate([m, m], axis=1))
        r = 1.0 / (jnp.broadcast_to(jnp.sum(e, axis=-1, keepdims=True), (nrow, LANES)) + jnp.exp(sink - m))
        eb = e.astype(bf16)
        v2 = vv[:, g * LANES:(g + 1) * LANES]
        o = (_dot(eb[:half], jnp.where(lo_v, v2, zero)) * r[:half]
             + _dot(eb[half:], jnp.where(lo_v, zero, v2)) * r[half:])
        for pr in range(pairs):
            c0 = (g * pairs + pr) * LANES
            o_ref[0, :, c0:c0 + LANES] = o[pr * BLOCK:(pr + 1) * BLOCK].astype(bf16)


def _swa(qa, ka2, va2, bias, sink):
    bsz, seq, _ = qa.shape
    nblk = seq // BLOCK
    cur = lambda w: pl.BlockSpec((1, BLOCK, w), lambda b, i: (b, i, 0))
    prev = lambda w: pl.BlockSpec((1, BLOCK, w), lambda b, i: (b, jnp.maximum(i - 1, 0), 0))
    return pl.pallas_call(
        _swa_kernel,
        grid=(bsz, nblk),
        in_specs=[
            cur(A_Q), prev(2 * A_KV), cur(2 * A_KV), prev(2 * A_KV), cur(2 * A_KV),
            pl.BlockSpec(bias.shape, lambda b, i: (0, 0, 0)),
            pl.BlockSpec(sink.shape, lambda b, i: (0, 0, 0)),
        ],
        out_specs=cur(A_Q),
        out_shape=jax.ShapeDtypeStruct((bsz, seq, A_Q), bf16),
        compiler_params=_cparams(("arbitrary", "arbitrary")),
        name="swa",
    )(qa, ka2, ka2, va2, va2, bias, sink)


def _chunk_pairs(body, lo, hi):
    n = hi - lo

    def two(k, carry):
        body(lo + 2 * k)
        body(lo + 2 * k + 1)
        return carry

    lax.fori_loop(0, lax.shift_right_logical(n, 1), two, 0)

    @pl.when((n & 1) == 1)
    def _():
        body(hi - 1)


def _dsa_kernel(qb_ref, qi_ref, wi_ref, ki_ref, ckv_ref, wuk_ref, wuv_ref, bias_ref, o_ref,
                qstk, qlat, wb, keys, keys_t, s_buf, p_buf, acc, mrun, lrun, *, topk, nch_total):
    i = pl.program_id(1)
    nch = i // 2 + 1
    rows_all = B_HEADS * BLOCK
    lo = lax.broadcasted_iota(jnp.int32, (BLOCK, LANES), 1) < HEAD_DIM
    zero = jnp.zeros((), bf16)

    qi = qi_ref[0]
    qb = qb_ref[0]
    for pr in range(B_HEADS // 2):
        qp = qi[:, pr * LANES:(pr + 1) * LANES]
        qstk[(2 * pr) * BLOCK:(2 * pr + 1) * BLOCK, :] = jnp.where(lo, qp, zero)
        qstk[(2 * pr + 1) * BLOCK:(2 * pr + 2) * BLOCK, :] = jnp.where(lo, zero, qp)
        qp = qb[:, pr * LANES:(pr + 1) * LANES]
        w = wuk_ref[pr]
        qlat[(2 * pr) * BLOCK:(2 * pr + 1) * BLOCK, :] = (
            _dot(jnp.where(lo, qp, zero), w) * (QK_SCALE * LOG2E)).astype(bf16)
        qlat[(2 * pr + 1) * BLOCK:(2 * pr + 2) * BLOCK, :] = (
            _dot(jnp.where(lo, zero, qp), w) * (QK_SCALE * LOG2E)).astype(bf16)
    wi = wi_ref[0]
    for h in range(IDX_HEADS):
        wb[h] = jnp.broadcast_to(wi[:, h:h + 1] * IDX_SCALE, (BLOCK, KCHUNK))

    qpos = i * BLOCK + lax.broadcasted_iota(jnp.int32, (BLOCK, KCHUNK), 0)
    kiota = lax.broadcasted_iota(jnp.int32, (BLOCK, KCHUNK), 1)

    def score_body(c):
        start = pl.multiple_of(c * KCHUNK, KCHUNK)
        d = _dot_nt(qstk[...], ki_ref[0, pl.ds(start, KCHUNK), :])
        s_buf[c] = _dot_nt(qlat[...], ckv_ref[0, pl.ds(start, KCHUNK), :])
        sc = jnp.zeros((BLOCK, KCHUNK), f32)
        for h in range(IDX_HEADS):
            sc = sc + wb[h] * jnp.maximum(d[h * BLOCK:(h + 1) * BLOCK], 0.0)
        sc = jnp.where(start + kiota <= qpos, sc, NEG_INF)
        keys[c] = _order_key(pltpu.bitcast(sc, jnp.int32))
        keys_t[c] = _order_key(pltpu.bitcast(sc.T, jnp.int32))

    _chunk_pairs(score_body, 0, nch)

    rest = ((nch_total - nch) * KCHUNK).astype(f32)
    part = 32

    def count_where(pred):
        def body(c, cnt):
            ind = jnp.where(pred(c, keys_t[c]), 1.0, 0.0)
            return cnt + jnp.sum(ind.reshape(KCHUNK // part, part, BLOCK), axis=0)
        cnt = lax.fori_loop(0, nch, body, jnp.zeros((part, BLOCK), f32))
        return jnp.sum(cnt, axis=0, keepdims=True)

    def count_ge(cand):
        return count_where(lambda c, k: k >= cand) + jnp.where(cand <= NEG_KEY, rest, 0.0)

    kf = jnp.float32(topk)
    seq_len = nch_total * KCHUNK
    cnt0 = count_ge(jnp.zeros((1, BLOCK), jnp.int32))
    tau0 = jnp.where(cnt0 >= kf, jnp.int32(0), jnp.int32(-2 ** 31))
    cnt0 = jnp.where(cnt0 >= kf, cnt0, jnp.float32(seq_len))

    def bisect_body(t, carry):
        tau, cnt = carry
        cand = tau + lax.shift_left(jnp.int32(1), 30 - t)
        c = count_ge(cand)
        return jnp.where(c >= kf, cand, tau), jnp.where(c >= kf, c, cnt)

    tau, n_ge = lax.fori_loop(0, 31, bisect_body, (tau0, cnt0))

    def to_rows(x):
        r = jnp.broadcast_to(x, (BLOCK, BLOCK)).T
        return jnp.concatenate([r, r], axis=1)

    tau_rows = to_rows(tau)

    tie = jnp.logical_and(n_ge > kf, tau > NEG_KEY)

    @pl.when(jnp.max(jnp.where(tie, 1.0, 0.0)) > 0.0)
    def _():
        need = kf - count_where(lambda c, k: k > tau)
        kpos_t = lax.broadcasted_iota(jnp.int32, (KCHUNK, BLOCK), 0)

        def eq_below(v):
            return count_where(lambda c, k: jnp.logical_and(k == tau, c * KCHUNK + kpos_t < v))

        def pos_body(t, v):
            cand = v + lax.shift_left(jnp.int32(1), (seq_len - 1).bit_length() - 1 - t)
            return jnp.where(eq_below(cand) < need, cand, v)

        v = lax.fori_loop(0, (seq_len - 1).bit_length(), pos_body, jnp.zeros((1, BLOCK), jnp.int32))
        jlim_rows = to_rows(jnp.where(tie, v + 1, seq_len))

        def drop_body(c, carry):
            kc = keys[c]
            late = jnp.logical_and(kc == tau_rows, c * KCHUNK + kiota >= jlim_rows)
            keys[c] = jnp.where(late, tau_rows - 1, kc)
            return carry

        lax.fori_loop(0, nch, drop_body, 0)

    mrun[...] = jnp.full(mrun.shape, NEG_INF, f32)
    nfar = jnp.maximum(nch - 2, 0)

    def pass1(c, near):
        start = pl.multiple_of(c * KCHUNK, KCHUNK)
        sel = keys[c] >= tau_rows
        if near:
            sel = jnp.logical_and(sel, start + kiota <= qpos)
            t0 = jnp.clip(2 * c - i + 2, 0, 2)
            t1 = jnp.clip(2 * c + 1 - i + 2, 0, 2)
        mb = jnp.where(sel, 0.0, NEG_INF)
        for h in range(B_HEADS):
            r0, r1 = h * BLOCK, (h + 1) * BLOCK
            s = s_buf[c, r0:r1, :] + mb
            if near:
                s = s + jnp.concatenate([bias_ref[t0, h], bias_ref[t1, h]], axis=1)
            s_buf[c, r0:r1, :] = s
            mrun[r0:r1, :] = jnp.maximum(mrun[r0:r1, :], jnp.maximum(s[:, :LANES], s[:, LANES:]))

    def pass1_far(c, carry):
        pass1(c, False)
        return carry

    def pass1_near(c, carry):
        pass1(c, True)
        return carry

    lax.fori_loop(0, nfar, pass1_far, 0)
    lax.fori_loop(nfar, nch, pass1_near, 0)

    for h in range(B_HEADS):
        r0, r1 = h * BLOCK, (h + 1) * BLOCK
        mrun[r0:r1, :] = jnp.broadcast_to(jnp.max(mrun[r0:r1, :], axis=-1, keepdims=True), (BLOCK, LANES))

    def pass2(c, first=False):
        start = c * KCHUNK if first else pl.multiple_of(c * KCHUNK, KCHUNK)
        for h in range(B_HEADS):
            r0, r1 = h * BLOCK, (h + 1) * BLOCK
            m = mrun[r0:r1, :]
            p = jnp.exp2(s_buf[c, r0:r1, :] - jnp.concatenate([m, m], axis=1))
            psum = p[:, :LANES] + p[:, LANES:]
            lrun[r0:r1, :] = psum if first else lrun[r0:r1, :] + psum
            p_buf[r0:r1, :] = p.astype(bf16)
        pv = _dot(p_buf[...], ckv_ref[0, pl.ds(start, KCHUNK), :])
        acc[...] = pv if first else acc[...] + pv

    pass2(0, first=True)
    _chunk_pairs(pass2, 1, nch)

    for pr in range(B_HEADS // 2):
        parts = []
        for par in (0, 1):
            r0 = (2 * pr + par) * BLOCK
            l = jnp.sum(lrun[r0:r0 + BLOCK, :], axis=-1, keepdims=True)
            parts.append((acc[r0:r0 + BLOCK, :] * (1.0 / l)).astype(bf16))
        o = _dot(jnp.concatenate(parts, axis=1), wuv_ref[pr])
        o_ref[0, :, pr * LANES:(pr + 1) * LANES] = o.astype(bf16)


def _dsa(qb, qi, wi, ki2, ckv, wuk_p, wuv_p, bias_t, l, topk):
    bsz, seq, _ = qb.shape
    nblk = seq // BLOCK
    nch_total = seq // KCHUNK
    rows_all = B_HEADS * BLOCK
    cur = lambda w: pl.BlockSpec((1, BLOCK, w), lambda b, i: (b, i, 0))
    whole = lambda w: pl.BlockSpec((1, seq, w), lambda b, i: (b, 0, 0))
    kern = functools.partial(_dsa_kernel, topk=topk, nch_total=nch_total)
    return pl.pallas_call(
        kern,
        grid=(bsz, nblk),
        in_specs=[
            cur(B_Q), cur(I_Q), cur(LANES), whole(2 * IDX_DIM), whole(KV_RANK),
            pl.BlockSpec((None,) + wuk_p.shape[1:], lambda b, i: (l, 0, 0, 0)),
            pl.BlockSpec((None,) + wuv_p.shape[1:], lambda b, i: (l, 0, 0, 0)),
            pl.BlockSpec(bias_t.shape, lambda b, i: (0, 0, 0, 0)),
        ],
        out_specs=cur(B_Q),
        out_shape=jax.ShapeDtypeStruct((bsz, seq, B_Q), bf16),
        scratch_shapes=[
            pltpu.VMEM((rows_all, LANES), bf16),
            pltpu.VMEM((rows_all, KV_RANK), bf16),
            pltpu.VMEM((IDX_HEADS, BLOCK, KCHUNK), f32),
            pltpu.VMEM((nch_total, BLOCK, KCHUNK), jnp.int32),
            pltpu.VMEM((nch_total, KCHUNK, BLOCK), jnp.int32),
            pltpu.VMEM((nch_total, rows_all, KCHUNK), f32),
            pltpu.VMEM((rows_all, KCHUNK), bf16),
            pltpu.VMEM((rows_all, KV_RANK), f32),
            pltpu.VMEM((rows_all, LANES), f32),
            pltpu.VMEM((rows_all, LANES), f32),
        ],
        compiler_params=_cparams(("arbitrary", "arbitrary")),
        name="dsa",
    )(qb, qi, wi, ki2, ckv, wuk_p, wuv_p, bias_t)


def _outproj_kernel(oa_ref, ob_ref, w_ref, x_ref, g_ref, o_ref):
    mix = _dot(oa_ref[...], w_ref[:A_Q, :]) + _dot(ob_ref[...], w_ref[A_Q:, :])
    o_ref[...] = x_ref[...] + g_ref[0] * mix


def _outproj(oa, ob, w_out, x2, g1, l, seq, tm):
    n, d = x2.shape
    rows = lambda w: pl.BlockSpec((tm, w), lambda i: (i, 0))
    return pl.pallas_call(
        _outproj_kernel,
        grid=(n // tm,),
        in_specs=[
            rows(A_Q), rows(B_Q),
            pl.BlockSpec((None, A_Q + B_Q, d), lambda i: (l, 0, 0), pipeline_mode=pl.Buffered(1)),
            rows(d),
            pl.BlockSpec((1, 1, d), lambda i: ((i * tm) // seq, 0, 0)),
        ],
        out_specs=rows(d),
        out_shape=jax.ShapeDtypeStruct((n, d), f32),
        compiler_params=_cparams(("arbitrary",)),
        name="outproj",
    )(oa, ob, w_out, x2, g1)


def _ffn_kernel(x_ref, xh_ref, sc_ref, sh_ref, gate_ref, g_ref, wu_ref, wg_ref, cw_ref, cb_ref, wd_ref, gf_ref,
                o_ref, hbuf, acc, *, seq, tm, final):
    i = pl.program_id(0)
    j = pl.program_id(1)

    @pl.when(j == 0)
    def _():
        hbuf[HALO:, :] = _rms_mod(x_ref[...], g_ref[...], sc_ref[0], sh_ref[0]).astype(bf16)
        hbuf[:HALO, :] = _rms_mod(xh_ref[...], g_ref[...], sc_ref[0], sh_ref[0]).astype(bf16)
        acc[...] = jnp.zeros(acc.shape, f32)

    u = _dot(hbuf[...], wu_ref[...])
    gt = _dot(hbuf[HALO:, :], wg_ref[...])
    tpos = lax.rem(i * tm, seq) + lax.broadcasted_iota(jnp.int32, (tm, 1), 0)
    u1 = jnp.where(tpos >= 1, u[HALO - 1:HALO - 1 + tm], 0.0)
    u2 = jnp.where(tpos >= 2, u[HALO - 2:HALO - 2 + tm], 0.0)
    cw = cw_ref[...]
    a = cw[2:3] * u[HALO:] + cw[1:2] * u1 + cw[0:1] * u2 + cb_ref[...]
    act = (a * _sigmoid(a) * gt).astype(bf16)
    acc[...] = acc[...] + _dot(act, wd_ref[...])

    @pl.when(j == pl.num_programs(1) - 1)
    def _():
        y = x_ref[...] + gate_ref[0] * acc[...]
        if final:
            y = y * lax.rsqrt(jnp.mean(y * y, axis=-1, keepdims=True) + EPS) * gf_ref[...]
        o_ref[...] = y


def _ffn(x2, sc, sh, gate, g, w_up, conv_w, conv_b, w_down, g_final, l, seq, tm, tf, final):
    n, d = x2.shape
    ff = w_down.shape[1]
    nf = ff // tf
    per_b = pl.BlockSpec((1, 1, d), lambda i, j: ((i * tm) // seq, 0, 0))
    halo_blocks = tm // HALO
    kern = functools.partial(_ffn_kernel, seq=seq, tm=tm, final=final)
    return pl.pallas_call(
        kern,
        grid=(n // tm, nf),
        in_specs=[
            pl.BlockSpec((tm, d), lambda i, j: (i, 0)),
            pl.BlockSpec((HALO, d), lambda i, j: (jnp.maximum(i * halo_blocks - 1, 0), 0)),
            per_b, per_b, per_b,
            pl.BlockSpec((None, 1, d), lambda i, j: (l, 0, 0)),
            pl.BlockSpec((None, d, tf), lambda i, j: (l, 0, j)),
            pl.BlockSpec((None, d, tf), lambda i, j: (l, 0, nf + j)),
            pl.BlockSpec((None, 3, tf), lambda i, j: (l, 0, j)),
            pl.BlockSpec((None, 1, tf), lambda i, j: (l, 0, j)),
            pl.BlockSpec((None, tf, d), lambda i, j: (l, j, 0)),
            pl.BlockSpec((1, d), lambda i, j: (0, 0)),
        ],
        out_specs=pl.BlockSpec((tm, d), lambda i, j: (i, 0)),
        out_shape=jax.ShapeDtypeStruct((n, d), f32),
        scratch_shapes=[pltpu.VMEM((tm + HALO, d), bf16), pltpu.VMEM((tm, d), f32)],
        compiler_params=_cparams(("arbitrary", "arbitrary")),
        name="ffn",
    )(x2, x2, sc, sh, gate, g, w_up, w_up, conv_w, conv_b, w_down, g_final)


def _rel_bucket(dist):
    n = jnp.maximum(dist, 0)
    max_exact = NUM_BUCKETS // 2
    nf = jnp.maximum(n, 1).astype(f32)
    large = max_exact + (jnp.log(nf / max_exact) / math.log(MAX_DISTANCE / max_exact)
                         * (NUM_BUCKETS - max_exact)).astype(jnp.int32)
    large = jnp.minimum(large, NUM_BUCKETS - 1)
    return jnp.where(n < max_exact, n, large)


def _pack_w_in(w_in):
    o_ka = A_Q
    o_va = o_ka + A_KV
    o_qb = o_va + A_KV
    o_ckv = o_qb + B_Q
    o_qi = o_ckv + KV_RANK
    o_ki = o_qi + I_Q
    o_wi = o_ki + IDX_DIM
    sl = lambda a, b: w_in[:, :, a:b]
    dup = lambda a: [sl(a, a + HEAD_DIM)] * 2
    ki = sl(o_ki, o_wi)
    wi = sl(o_wi, o_wi + IDX_HEADS)
    pad = jnp.zeros(w_in.shape[:2] + (LANES - IDX_HEADS,), w_in.dtype)
    cols = ([sl(0, A_Q), sl(o_qb, o_ckv), sl(o_qi, o_ki), sl(o_ckv, o_qi)]
            + dup(o_ka) + dup(o_ka + HEAD_DIM) + dup(o_va) + dup(o_va + HEAD_DIM)
            + [ki, ki, wi, pad])
    return jnp.concatenate(cols, axis=-1).astype(bf16)


def _swa_tables(rel_a, sinks):
    t = jnp.arange(BLOCK, dtype=jnp.int32)[:, None]
    j = jnp.arange(2 * BLOCK, dtype=jnp.int32)[None, :]
    dist = t + BLOCK - j
    valid = (dist >= 0) & (dist < BLOCK)
    tab = jnp.where(valid[None], jnp.transpose(rel_a[_rel_bucket(dist)], (2, 0, 1)), NEG_INF)
    pairs = A_HEADS // A_KV_HEADS // 2
    order = [g * 2 * pairs + 2 * pr + par for g in range(A_KV_HEADS) for par in (0, 1) for pr in range(pairs)]
    tab = tab[jnp.array(order)].reshape(A_KV_HEADS, 2 * pairs * BLOCK, 2 * BLOCK).astype(f32)
    sink = jnp.repeat(sinks[:, jnp.array(order)], BLOCK, axis=1).reshape(-1, A_KV_HEADS, 2 * pairs * BLOCK, 1)
    return tab, jnp.broadcast_to(sink, sink.shape[:-1] + (LANES,)).astype(f32)


def _dsa_tables(rel_b):
    t = jnp.arange(BLOCK, dtype=jnp.int32)[:, None]
    j = jnp.arange(BLOCK, dtype=jnp.int32)[None, :]
    far = rel_b[NUM_BUCKETS - 1][:, None, None]
    prev = jnp.transpose(rel_b[_rel_bucket(t + BLOCK - j)], (2, 0, 1)) - far
    diag = jnp.transpose(rel_b[_rel_bucket(t - j)], (2, 0, 1)) - far
    return (jnp.stack([jnp.zeros_like(prev), prev, diag]) * LOG2E).astype(f32)


def _pick(n, prefs):
    for p in prefs:
        if n % p == 0:
            return p
    return n


def kernel(x, c, w_mod, b_mod, norm_attn, w_in, attn_sinks, kv_norm, w_uk, w_uv, rel_bias, w_out,
           norm_ffn, w_up, conv_w, conv_b, w_down, norm_final):
    bsz, seq, d = x.shape
    depth = w_in.shape[0]
    ff = w_down.shape[1]
    n = bsz * seq
    assert seq % KCHUNK == 0 and d % LANES == 0 and ff % LANES == 0
    topk = min(TOPK_MAX, seq // 4)
    tm = _pick(seq, (512, 256, 128))
    tf = _pick(ff, (512, 256, 128))

    w_in_p = _pack_w_in(w_in)
    w_out_b = w_out.astype(bf16)
    w_up_b = w_up.astype(bf16)
    w_down_b = w_down.astype(bf16)
    wuk_p = jnp.swapaxes(w_uk, 2, 3).reshape(depth, B_HEADS // 2, 2 * HEAD_DIM, KV_RANK).astype(bf16)
    z = jnp.zeros_like(w_uv[:, 0::2])
    wuv_p = jnp.concatenate([jnp.concatenate([w_uv[:, 0::2], z], axis=-1),
                             jnp.concatenate([z, w_uv[:, 1::2]], axis=-1)], axis=2).astype(bf16)
    swa_bias, swa_sink = _swa_tables(rel_bias[:, :A_HEADS], attn_sinks)
    dsa_bias = _dsa_tables(rel_bias[:, A_HEADS:])
    norm_attn3 = norm_attn.reshape(depth, 1, d)
    norm_ffn3 = norm_ffn.reshape(depth, 1, d)
    kv_norm3 = kv_norm.reshape(depth, 1, KV_RANK)
    conv_b3 = conv_b.reshape(depth, 1, ff)

    mod = _modulation(c, w_mod, b_mod)
    x2 = x.reshape(n, d)
    for l in range(depth):
        sh1, sc1, g1, sh2, sc2, g2 = [mod[l, :, k * d:(k + 1) * d].reshape(bsz, 1, d) for k in range(6)]
        qa, qb, qi, ckv, ka2, va2, ki2, wi = _inproj(x2, sc1, sh1, norm_attn3, w_in_p, kv_norm3, l, seq, tm)
        r3 = lambda a: a.reshape(bsz, seq, a.shape[-1])
        out_a = _swa(r3(qa), r3(ka2), r3(va2), swa_bias, swa_sink[l])
        out_b = _dsa(r3(qb), r3(qi), r3(wi), r3(ki2), r3(ckv), wuk_p, wuv_p, dsa_bias, l, topk)
        x2 = _outproj(out_a.reshape(n, A_Q), out_b.reshape(n, B_Q), w_out_b, x2, g1, l, seq, tm)
        x2 = _ffn(x2, sc2, sh2, g2, norm_ffn3, w_up_b, conv_w, conv_b3, w_down_b, norm_final.reshape(1, d),
                  l, seq, tm, tf, final=(l == depth - 1))
    return x2.reshape(bsz, seq, d)
```
